```python
import math
import jax, jax.numpy as jnp
from jax import lax
import numpy as np

D_MODEL = 1024
BATCH = 2
SEQ = 8192
DEPTH = 4
DEC_BATCH = 128
DEC_SEQ = 8
PAST_LEN = 2048
PAGE_SIZE = 128

MIX = D_MODEL
WA = MIX // 2
DKA = 128
DVA = 128
HA = WA // DKA
WB = MIX // 4
DHB = 64
HB = WB // DHB
WC = MIX - WA - WB
DHC = 64
HC = WC // DHC
N_IN = 4 * WA + 4 * WB + HB + 4 * WC
Q_BLOCK = 128
HGRN_CHUNK = 32
EPS = 1e-6
FOX_BIAS_INIT = 3.0

kernel_name = "hymba_hgrn2_fox_stickbreak_decode_step"

F32 = jnp.float32


def rmsnorm(x, g):
    xf = x.astype(F32)
    y = xf * lax.rsqrt(jnp.mean(xf * xf, axis=-1, keepdims=True) + EPS)
    return (y * g.astype(F32)).astype(x.dtype)


def head_rms(o):
    of = o.astype(F32)
    return of * lax.rsqrt(jnp.mean(of * of, axis=-1, keepdims=True) + EPS)


def hgrn2_recurrence(q, k, v, log_f, s0):
    B, L, H, DK = q.shape
    DV = v.shape[-1]
    c = math.gcd(L, HGRN_CHUNK)
    n = L // c

    def blk(a):
        return a.astype(F32).reshape(B, n, c, H, a.shape[-1]).transpose(1, 0, 3, 2, 4)

    q, k, v, lf = blk(q), blk(k), blk(v), blk(log_f)
    b = jnp.cumsum(lf, axis=3)
    b_last = b[:, :, :, -1:, :]
    q_in = q * jnp.exp(b)
    k_in = k * jnp.exp(-b)
    k_out = k * jnp.exp(b_last - b)
    causal = jnp.tril(jnp.ones((c, c), dtype=bool))
    a = jnp.where(causal, jnp.einsum('nbhtd,nbhsd->nbhts', q_in, k_in), 0.0)
    o_intra = jnp.einsum('nbhts,nbhsv->nbhtv', a, v)
    decay = jnp.exp(b_last[:, :, :, 0, :])

    def step(s, xs):
        q_i, k_i, v_i, d_i = xs
        o_i = jnp.einsum('bhtd,bhdv->bhtv', q_i, s)
        s = d_i[..., None] * s + jnp.einsum('bhsd,bhsv->bhdv', k_i, v_i)
        return s, o_i

    s_fin, o_inter = lax.scan(step, s0.astype(F32), (q_in, k_out, v, decay))
    o = (o_intra + o_inter).transpose(1, 0, 3, 2, 4).reshape(B, L, H, DV)
    return o, s_fin


def query_blocks(fn, q_args, qpos):
    L = qpos.shape[0]
    blk = Q_BLOCK if L % Q_BLOCK == 0 else L
    n = L // blk

    def split(a):
        return jnp.moveaxis(a.reshape(a.shape[0], n, blk, *a.shape[2:]), 1, 0)

    xs = tuple(split(a) for a in q_args) + (qpos.reshape(n, blk),)
    out = lax.map(lambda t: fn(*t), xs)
    return jnp.moveaxis(out, 0, 1).reshape(out.shape[1], L, *out.shape[3:])


def fox_block(q, cq, qpos, k, v, ck, kpos):
    s = jnp.einsum('bqhd,bkhd->bhqk', q, k).astype(F32) * (DHB ** -0.5)
    s = s + jnp.swapaxes(cq, 1, 2)[..., :, None] - jnp.swapaxes(ck, 1, 2)[..., None, :]
    s = jnp.where(kpos[None, :] <= qpos[:, None], s, -jnp.inf)
    p = jax.nn.softmax(s, axis=-1)
    return jnp.einsum('bhqk,bkhd->bqhd', p.astype(v.dtype), v)


def sb_block(q, qpos, k, v, kpos):
    z = jnp.einsum('bqhd,bkhd->bhqk', q, k).astype(F32) * (DHC ** -0.5)
    mask = kpos[None, :] < qpos[:, None]
    log_keep = jnp.where(mask, jax.nn.log_sigmoid(-z), 0.0)
    after = lax.cumsum(log_keep, axis=3, reverse=True) - log_keep
    w = jnp.where(mask, jnp.exp(jax.nn.log_sigmoid(z) + after), 0.0)
    return jnp.einsum('bhqk,bkhd->bqhd', w.astype(v.dtype), v)


def mixer(h, w_in_l, b_fox_l, lb, g_out_l, w_out_l, s0, pk_b, pv_b, plf_b, pk_c, pv_c):
    B, L, _ = h.shape
    P = pk_b.shape[1]
    u = h @ w_in_l
    splits = [int(s) for s in np.cumsum([WA, WA, WA, WA, WB, WB, WB, WB, HB, WC, WC, WC])]
    qa, fa, ia, ga, qb, kb, vb, gb, fb, qc, kc, vc, gc = jnp.split(u, splits, axis=-1)

    za = fa.astype(F32).reshape(B, L, HA, DKA)
    lb_h = lb.reshape(HA, DKA)
    log_f = jnp.logaddexp(jnp.log(lb_h), jnp.log1p(-lb_h) + jax.nn.log_sigmoid(za))
    k_a = (1.0 - lb_h) * jax.nn.sigmoid(-za)
    o_a, s_fin = hgrn2_recurrence(jax.nn.silu(qa).reshape(B, L, HA, DKA), k_a,
                                  ia.reshape(B, L, HA, DVA), log_f, s0)

    qpos = P + jnp.arange(L)
    kpos = jnp.arange(P + L)

    qb = qb.reshape(B, L, HB, DHB)
    kb = kb.reshape(B, L, HB, DHB)
    vb = vb.reshape(B, L, HB, DHB)
    lf_b = jax.nn.log_sigmoid(fb.astype(F32) + b_fox_l.astype(F32))
    kb_all = jnp.concatenate([pk_b.astype(kb.dtype), kb], axis=1)
    vb_all = jnp.concatenate([pv_b.astype(vb.dtype), vb], axis=1)
    c_all = jnp.cumsum(jnp.concatenate([plf_b.astype(F32), lf_b], axis=1), axis=1)
    o_b = query_blocks(lambda q, cq, qp: fox_block(q, cq, qp, kb_all, vb_all, c_all, kpos),
                       (qb, c_all[:, P:]), qpos)

    qc = qc.reshape(B, L, HC, DHC)
    kc = kc.reshape(B, L, HC, DHC)
    vc = vc.reshape(B, L, HC, DHC)
    kc_all = jnp.concatenate([pk_c.astype(kc.dtype), kc], axis=1)
    vc_all = jnp.concatenate([pv_c.astype(vc.dtype), vc], axis=1)
    o_c = query_blocks(lambda q, qp: sb_block(q, qp, kc_all, vc_all, kpos), (qc,), qpos)

    o = jnp.concatenate([head_rms(o_a).reshape(B, L, WA), head_rms(o_b).reshape(B, L, WB),
                         head_rms(o_c).reshape(B, L, WC)], axis=-1)
    gate = jax.nn.silu(jnp.concatenate([ga, gb, gc], axis=-1).astype(F32))
    o = o * g_out_l.astype(F32) * gate
    y = o.astype(h.dtype) @ w_out_l
    return y, s_fin.astype(h.dtype), kb, vb, lf_b.astype(h.dtype), kc, vc


def setup_inputs(seed: int = 0) -> dict:
    key = jax.random.key(seed)
    ks = jax.random.split(key, 16)
    n_pages = PAST_LEN // PAGE_SIZE
    used = DEC_BATCH * n_pages
    n_phys = used + max(1, used // 4)
    page_table = jax.random.permutation(ks[0], n_phys)[:used].reshape(DEC_BATCH, n_pages).astype(jnp.int32)
    nrm = jax.random.normal
    return {
        "x_prompt": nrm(ks[1], (BATCH, SEQ, D_MODEL), F32),
        "x_sample": nrm(ks[2], (DEC_BATCH, DEC_SEQ, D_MODEL), F32),
        "cache_fox_k": nrm(ks[3], (DEPTH, n_phys, PAGE_SIZE, HB, DHB), F32),
        "cache_fox_v": nrm(ks[4], (DEPTH, n_phys, PAGE_SIZE, HB, DHB), F32),
        "cache_fox_logf": jax.nn.log_sigmoid(FOX_BIAS_INIT + 0.5 * nrm(ks[5], (DEPTH, n_phys, PAGE_SIZE, HB), F32)),
        "cache_sb_k": nrm(ks[6], (DEPTH, n_phys, PAGE_SIZE, HC, DHC), F32),
        "cache_sb_v": nrm(ks[7], (DEPTH, n_phys, PAGE_SIZE, HC, DHC), F32),
        "state_hgrn": 0.5 * nrm(ks[8], (DEPTH, DEC_BATCH, HA, DKA, DVA), F32),
        "page_table": page_table,
        "w_in": nrm(ks[9], (DEPTH, D_MODEL, N_IN), F32) * D_MODEL ** -0.5,
        "b_fox": FOX_BIAS_INIT + 0.5 * nrm(ks[10], (DEPTH, HB), F32),
        "lb_param": 0.5 * nrm(ks[11], (DEPTH, WA), F32),
        "w_out": nrm(ks[12], (DEPTH, MIX, D_MODEL), F32) * MIX ** -0.5,
        "g_pre": 1.0 + 0.1 * nrm(ks[13], (DEPTH, D_MODEL), F32),
        "g_post": 1.0 + 0.1 * nrm(ks[14], (DEPTH, D_MODEL), F32),
        "g_out": 1.0 + 0.1 * nrm(ks[15], (DEPTH, MIX), F32),
    }


def reference(x_prompt, x_sample, cache_fox_k, cache_fox_v, cache_fox_logf, cache_sb_k, cache_sb_v,
              state_hgrn, page_table, w_in, b_fox, lb_param, w_out, g_pre, g_post, g_out):
    lbs = jnp.cumsum(jax.nn.softmax(lb_param.astype(F32), axis=0), axis=0)
    lbs = lbs - lbs[0:1]

    def gather(pool):
        g = pool[page_table]
        return g.reshape(g.shape[0], g.shape[1] * g.shape[2], *g.shape[3:])

    xp, xs = x_prompt, x_sample
    Bp = xp.shape[0]
    dt = xp.dtype
    p_out = [[] for _ in range(6)]
    s_out = [[] for _ in range(6)]
    for l in range(DEPTH):
        hp = rmsnorm(xp, g_pre[l])
        yp, sp, kbp, vbp, lfp, kcp, vcp = mixer(
            hp, w_in[l], b_fox[l], lbs[l], g_out[l], w_out[l],
            jnp.zeros((Bp, HA, DKA, DVA), dt),
            jnp.zeros((Bp, 0, HB, DHB), dt), jnp.zeros((Bp, 0, HB, DHB), dt), jnp.zeros((Bp, 0, HB), dt),
            jnp.zeros((Bp, 0, HC, DHC), dt), jnp.zeros((Bp, 0, HC, DHC), dt))
        xp = xp + rmsnorm(yp, g_post[l])
        for lst, a in zip(p_out, (kbp, vbp, lfp, kcp, vcp, sp)):
            lst.append(a)

        hs = rmsnorm(xs, g_pre[l])
        ys, ss, kbs, vbs, lfs, kcs, vcs = mixer(
            hs, w_in[l], b_fox[l], lbs[l], g_out[l], w_out[l], state_hgrn[l],
            gather(cache_fox_k[l]), gather(cache_fox_v[l]), gather(cache_fox_logf[l]),
            gather(cache_sb_k[l]), gather(cache_sb_v[l]))
        xs = xs + rmsnorm(ys, g_post[l])
        for lst, a in zip(s_out, (kbs, vbs, lfs, kcs, vcs, ss)):
            lst.append(a)

    fox_k_p, fox_v_p, fox_lf_p, sb_k_p, sb_v_p, hgrn_p = [jnp.stack(a, axis=0) for a in p_out]
    fox_k_s, fox_v_s, fox_lf_s, sb_k_s, sb_v_s, hgrn_s = [jnp.stack(a, axis=0) for a in s_out]
    return (xp, xs, fox_k_p, fox_v_p, fox_lf_p, sb_k_p, sb_v_p, hgrn_p,
            fox_k_s, fox_v_s, fox_lf_s, sb_k_s, sb_v_s, hgrn_s)
```

```python
import functools

import jax
import jax.numpy as jnp
from jax import lax
from jax.experimental import pallas as pl
from jax.experimental.pallas import tpu as pltpu

F32 = jnp.float32
BF16 = jnp.bfloat16

EPS = 1e-6
DKA = 128
DVA = 128
DH = 64
HGRN_CHUNK = 32
NEG_BIG = -1e30
SB_DEAD = -104.0

ROW_TILE = 512
SB_SUB = 256
VMEM_LIMIT = 56 * 1024 * 1024

NT = (((1,), (1,)), ((), ()))
TN = (((0,), (0,)), ((), ()))


def _params(sem):
    return pltpu.CompilerParams(dimension_semantics=sem, vmem_limit_bytes=VMEM_LIMIT)


def _softplus(x):
    return jnp.maximum(x, 0.0) + jnp.log1p(jnp.exp(-jnp.abs(x)))


def _log_sigmoid(x):
    return -_softplus(-x)


def _sigmoid(x):
    return 1.0 / (1.0 + jnp.exp(-x))


def _split_bf16(x, n):
    parts = []
    r = x
    for t in range(n):
        p = r.astype(BF16)
        parts.append(p)
        if t + 1 < n:
            r = r - p.astype(F32)
    return parts


def _dot_exact_lhs(x, m01, n=3):
    acc = None
    for p in _split_bf16(x, n):
        t = jnp.dot(p, m01, preferred_element_type=F32)
        acc = t if acc is None else acc + t
    return acc


def _dot_exact_rhs(m01, x, n=3):
    acc = None
    for p in _split_bf16(x, n):
        t = jnp.dot(m01, p, preferred_element_type=F32)
        acc = t if acc is None else acc + t
    return acc


def _iota2(shape, dim):
    return lax.broadcasted_iota(jnp.int32, shape, dim)


def _mask01(cond):
    return jnp.where(cond, 1.0, 0.0).astype(BF16)


def _div_pow2(x, d):
    assert d & (d - 1) == 0
    return lax.shift_right_logical(x, d.bit_length() - 1)


def _mod_pow2(x, d):
    assert d & (d - 1) == 0
    return lax.bitwise_and(x, d - 1)


def _in_proj_body(x_ref, gpre_ref, w_ref, wf_ref, bf_ref, gout_ref, ua_ref, gate_ref, *out_refs,
                  wa, wb, wc, mix, prompt):
    x = x_ref[...]
    h = x * lax.rsqrt(jnp.mean(x * x, axis=-1, keepdims=True) + EPS) * gpre_ref[...]
    hb = h.astype(BF16)

    def proj(lo, hi):
        return jnp.dot(hb, w_ref[:, lo:hi], preferred_element_type=F32)

    ua_ref[...] = proj(0, 3 * wa)
    g = proj(3 * wa, 3 * wa + mix)
    gate_ref[...] = g * _sigmoid(g) * gout_ref[...]

    def attn_group(off, w, refs):
        u = proj(off, off + 3 * w)
        q = u[:, :w] * (DH ** -0.5)
        k = u[:, w:2 * w]
        v = u[:, 2 * w:3 * w]
        if not prompt:
            q32_ref, k32_ref, v32_ref = refs
            q32_ref[...] = q
            k32_ref[...] = k
            v32_ref[...] = v
            return
        qh_ref, kt_ref, vh_ref, kt32_ref, vt32_ref = refs
        kt = k.T
        vt = v.T
        for hh in range(w // DH):
            qh_ref[hh] = q[:, hh * DH:(hh + 1) * DH].astype(BF16)
            vh_ref[hh] = v[:, hh * DH:(hh + 1) * DH].astype(BF16)
            kt_ref[hh, 0] = kt[hh * DH:(hh + 1) * DH, :].astype(BF16)
            kt32_ref[0, hh] = kt[hh * DH:(hh + 1) * DH, :]
            vt32_ref[0, hh] = vt[hh * DH:(hh + 1) * DH, :]

    per_group = 5 if prompt else 3
    off_b = 3 * wa + mix
    attn_group(off_b, wb, out_refs[0:per_group])
    attn_group(off_b + 3 * wb, wc, out_refs[per_group:2 * per_group])

    lf_ref = out_refs[2 * per_group]
    if prompt:
        f = lax.dot_general(wf_ref[...], hb, NT, preferred_element_type=F32)
    else:
        f = jnp.dot(hb, wf_ref[...], preferred_element_type=F32)
    lf_ref[...] = _log_sigmoid(f + bf_ref[...])


def _in_proj(x2d, gpre, w, wf, bf, gout, *, wa, wb, wc, prompt, nseq=1):
    r, d = x2d.shape
    tm = ROW_TILE
    mix = wa + wb + wc
    n = r // tm
    per_seq = n // nseq
    seqlen = r // nseq
    full = lambda shape: pl.BlockSpec(shape, lambda i: (0,) * len(shape))
    rows = lambda width: pl.BlockSpec((tm, width), lambda i: (i, 0))
    out_shape = [jax.ShapeDtypeStruct((r, 3 * wa), F32),
                 jax.ShapeDtypeStruct((r, mix), F32)]
    out_specs = [rows(3 * wa), rows(mix)]
    for w_g in (wb, wc):
        nh = w_g // DH
        if prompt:
            heads = pl.BlockSpec((nh, tm, DH), lambda i: (0, i, 0))
            tposed = pl.BlockSpec((1, nh, DH, tm), lambda i: (i // per_seq, 0, 0, i % per_seq))
            out_shape += [jax.ShapeDtypeStruct((nh, r, DH), BF16),
                          jax.ShapeDtypeStruct((nh, n, DH, tm), BF16),
                          jax.ShapeDtypeStruct((nh, r, DH), BF16),
                          jax.ShapeDtypeStruct((nseq, nh, DH, seqlen), F32),
                          jax.ShapeDtypeStruct((nseq, nh, DH, seqlen), F32)]
            out_specs += [heads, pl.BlockSpec((nh, 1, DH, tm), lambda i: (0, i, 0, 0)), heads, tposed, tposed]
        else:
            out_shape += [jax.ShapeDtypeStruct((r, w_g), F32)] * 3
            out_specs += [rows(w_g)] * 3
    if prompt:
        out_shape.append(jax.ShapeDtypeStruct((16, r), F32))
        out_specs.append(pl.BlockSpec((16, tm), lambda i: (0, i)))
    else:
        out_shape.append(jax.ShapeDtypeStruct((r, 128), F32))
        out_specs.append(rows(128))
    in_specs = [rows(d), full((1, d)), full(w.shape), full(wf.shape), full(bf.shape), full((1, mix))]
    return pl.pallas_call(
        functools.partial(_in_proj_body, wa=wa, wb=wb, wc=wc, mix=mix, prompt=prompt),
        grid=(n,), in_specs=in_specs, out_specs=out_specs, out_shape=out_shape,
        compiler_params=_params(("parallel",)), name="in_proj_prompt" if prompt else "in_proj_sample",
    )(x2d, gpre, w, wf, bf, gout)


def _cum_body(lf_ref, c_ref, carry_ref, *, tc):
    @pl.when(pl.program_id(1) == 0)
    def _():
        carry_ref[...] = jnp.zeros_like(carry_ref)

    upper = _mask01(_iota2((tc, tc), 0) <= _iota2((tc, tc), 1))
    c = _dot_exact_lhs(lf_ref[...], upper) + carry_ref[...]
    c_ref[0] = c
    carry_ref[...] = c[:, tc - 1:tc]


def _cum(lft, nseq, seqlen):
    tc = ROW_TILE
    nb = seqlen // tc
    return pl.pallas_call(
        functools.partial(_cum_body, tc=tc),
        grid=(nseq, nb),
        in_specs=[pl.BlockSpec((8, tc), lambda b, i: (0, b * nb + i))],
        out_specs=pl.BlockSpec((1, 8, tc), lambda b, i: (b * nb + i, 0, 0)),
        out_shape=jax.ShapeDtypeStruct((nseq * nb, 8, tc), F32),
        scratch_shapes=[pltpu.VMEM((8, 1), F32)],
        compiler_params=_params(("parallel", "arbitrary")), name="fox_cum",
    )(lft)


def _hgrn_body(q_ref, f_ref, i_ref, lbp_ref, s0_ref, o_ref, sout_ref,
               st_ref, qin_ref, kin_ref, kout_ref, dec_ref,
               *, layer, nseq, nchunk, c):
    step = pl.program_id(2)
    t = nchunk * c
    rows = nseq * t

    @pl.when(step == 0)
    def _():
        for s in range(nseq):
            st_ref[s] = s0_ref[0, s, 0].T

    lbp = lbp_ref[...]
    e = jnp.exp(lbp - jnp.max(lbp, axis=0, keepdims=True))
    sm = e / jnp.sum(e, axis=0, keepdims=True)
    lb = jnp.zeros((1, DKA), F32)
    for r in range(1, layer + 1):
        lb = lb + sm[r:r + 1, :]

    z = f_ref[...]
    qa = q_ref[...]
    q = qa * _sigmoid(qa)
    log_f = jnp.log(lb)
    other = jnp.log1p(-lb) + _log_sigmoid(z)
    mx = jnp.maximum(log_f, other)
    log_f = mx + jnp.log1p(jnp.exp(-jnp.abs(log_f - other)))
    k = (1.0 - lb) * _sigmoid(-z)

    ri = _iota2((rows, rows), 0)
    ci = _iota2((rows, rows), 1)
    same = _div_pow2(ri, c) == _div_pow2(ci, c)
    b = _dot_exact_rhs(_mask01(same & (ci <= ri)), log_f)
    b_last = _dot_exact_rhs(_mask01(same), log_f)
    qin_ref[...] = q * jnp.exp(b)
    kin_ref[...] = k * jnp.exp(-b)
    kout_ref[...] = k * jnp.exp(b_last - b)
    dec_ref[...] = jnp.exp(b_last)

    causal = _iota2((c, c), 1) <= _iota2((c, c), 0)
    for s in range(nseq):
        st = st_ref[s]
        for ch in range(nchunk):
            lo = s * t + ch * c
            qi = qin_ref[lo:lo + c, :].astype(BF16)
            ki = kin_ref[lo:lo + c, :].astype(BF16)
            ko = kout_ref[lo:lo + c, :].astype(BF16)
            vv = i_ref[lo:lo + c, :].astype(BF16)
            a = lax.dot_general(qi, ki, NT, preferred_element_type=F32)
            a = jnp.where(causal, a, 0.0)
            o = jnp.dot(a.astype(BF16), vv, preferred_element_type=F32)
            o = o + lax.dot_general(qi, st.astype(BF16), NT, preferred_element_type=F32)
            st = st * dec_ref[lo:lo + 1, :] + lax.dot_general(vv, ko, TN, preferred_element_type=F32)
            o_ref[lo:lo + c, :] = o * lax.rsqrt(jnp.mean(o * o, axis=-1, keepdims=True) + EPS)
        st_ref[s] = st
        sout_ref[s, 0] = st.T


def _hgrn(ua, lb_param, s0, layer, s0_layer, *, nseq_total, seqlen, nseq, nchunk, c, ha):
    r = ua.shape[0]
    t = nchunk * c
    nsteps = seqlen // t
    rows = nseq * t
    depth = lb_param.shape[0]
    col = lambda off: pl.BlockSpec((rows, DKA), lambda sb, h, i: (sb * nsteps + i, off + h))
    return pl.pallas_call(
        functools.partial(_hgrn_body, layer=layer, nseq=nseq, nchunk=nchunk, c=c),
        grid=(nseq_total // nseq, ha, nsteps),
        in_specs=[col(0), col(ha), col(2 * ha),
                  pl.BlockSpec((depth, DKA), lambda sb, h, i: (0, h)),
                  pl.BlockSpec((1, nseq, 1, DKA, DVA), lambda sb, h, i: (s0_layer, sb, h, 0, 0))],
        out_specs=(pl.BlockSpec((rows, DVA), lambda sb, h, i: (sb * nsteps + i, h)),
                   pl.BlockSpec((nseq, 1, DKA, DVA), lambda sb, h, i: (sb, h, 0, 0))),
        out_shape=(jax.ShapeDtypeStruct((r, ha * DVA), F32),
                   jax.ShapeDtypeStruct((nseq_total, ha, DKA, DVA), F32)),
        scratch_shapes=[pltpu.VMEM((nseq, DVA, DKA), F32)] + [pltpu.VMEM((rows, DKA), F32)] * 4,
        compiler_params=_params(("parallel", "parallel", "arbitrary")), name="hgrn",
    )(ua, ua, ua, lb_param, s0)


def _head_rms(o):
    return o * lax.rsqrt(jnp.mean(o * o, axis=-1, keepdims=True) + EPS)


def _fox_body(q_ref, kt_ref, v_ref, ct_ref, o_ref, m_ref, l_ref, acc_ref, *, tq):
    h = pl.program_id(1)
    i = pl.program_id(2)
    q = q_ref[0]

    m_ref[...] = jnp.full_like(m_ref, NEG_BIG)
    l_ref[...] = jnp.zeros_like(l_ref)
    acc_ref[...] = jnp.zeros_like(acc_ref)

    def block(j, masked):
        kt = kt_ref[0, j]
        v = v_ref[0, pl.ds(pl.multiple_of(j * tq, tq), tq), :]
        s = jnp.dot(q, kt, preferred_element_type=F32) - ct_ref[j, pl.ds(h, 1), :]
        if masked:
            s = jnp.where(_iota2((tq, tq), 1) <= _iota2((tq, tq), 0), s, -jnp.inf)
        m_old = m_ref[...]
        m_new = jnp.maximum(m_old, jnp.max(s, axis=-1, keepdims=True))
        alpha = jnp.exp(m_old - m_new)
        p = jnp.exp(s - m_new)
        l_ref[...] = alpha * l_ref[...] + jnp.sum(p, axis=-1, keepdims=True)
        acc_ref[...] = alpha * acc_ref[...] + jnp.dot(p.astype(BF16), v, preferred_element_type=F32)
        m_ref[...] = m_new

    block(i, True)

    def body(jj, carry):
        block(i - 1 - jj, False)
        return carry

    lax.fori_loop(0, i, body, 0)
    o_ref[0] = _head_rms(acc_ref[...] / l_ref[...])


def _fox(qh, kt, vh, ct, *, nseq, seqlen):
    nh, r, _ = qh.shape
    tq = ROW_TILE
    nq = seqlen // tq
    return pl.pallas_call(
        functools.partial(_fox_body, tq=tq),
        grid=(nseq, nh, nq),
        in_specs=[pl.BlockSpec((1, tq, DH), lambda b, h, i: (h, b * nq + i, 0)),
                  pl.BlockSpec((1, nq, DH, tq), lambda b, h, i: (h, b, 0, 0)),
                  pl.BlockSpec((1, seqlen, DH), lambda b, h, i: (h, b, 0)),
                  pl.BlockSpec((nq, 8, tq), lambda b, h, i: (b, 0, 0))],
        out_specs=pl.BlockSpec((1, tq, DH), lambda b, h, i: (h, b * nq + i, 0)),
        out_shape=jax.ShapeDtypeStruct((nh, r, DH), F32),
        scratch_shapes=[pltpu.VMEM((tq, 1), F32), pltpu.VMEM((tq, 1), F32), pltpu.VMEM((tq, DH), F32)],
        compiler_params=_params(("parallel", "parallel", "parallel")), name="fox",
    )(qh, kt, vh, ct)


def _sb_body(q_ref, kt_ref, v_ref, o_ref, carry_ref, acc_ref, *, tq):
    i = pl.program_id(2)
    q = q_ref[0]
    ts = SB_SUB
    nsub = tq // ts
    later = _mask01(_iota2((ts, ts), 0) > _iota2((ts, ts), 1))

    carry_ref[...] = jnp.zeros_like(carry_ref)
    acc_ref[...] = jnp.zeros_like(acc_ref)

    def block(j, masked):
        kt = kt_ref[0, j]
        for sub in reversed(range(nsub)):
            v = v_ref[0, pl.ds(pl.multiple_of(j * tq + sub * ts, ts), ts), :]
            z = jnp.dot(q, kt[:, sub * ts:(sub + 1) * ts], preferred_element_type=F32)
            sp = _softplus(z)
            if masked:
                mask = (_iota2((tq, ts), 1) + sub * ts) < _iota2((tq, ts), 0)
                log_keep = jnp.where(mask, -sp, 0.0)
            else:
                log_keep = -sp
            after = _dot_exact_lhs(log_keep, later, n=2) + carry_ref[...]
            w = jnp.exp(z - sp + after)
            if masked:
                w = jnp.where(mask, w, 0.0)
            acc_ref[...] += jnp.dot(w.astype(BF16), v, preferred_element_type=F32)
            carry_ref[...] += jnp.sum(log_keep, axis=-1, keepdims=True)

    block(i, True)

    def cond(c):
        j, live = c
        return jnp.logical_and(j >= 0, live > SB_DEAD)

    def body(c):
        j, _ = c
        block(j, False)
        return j - 1, jnp.max(carry_ref[...])

    lax.while_loop(cond, body, (i - 1, jnp.max(carry_ref[...])))
    o_ref[0] = _head_rms(acc_ref[...])


def _sb(qh, kt, vh, *, nseq, seqlen):
    nh, r, _ = qh.shape
    tq = ROW_TILE
    nq = seqlen // tq
    return pl.pallas_call(
        functools.partial(_sb_body, tq=tq),
        grid=(nseq, nh, nq),
        in_specs=[pl.BlockSpec((1, tq, DH), lambda b, h, i: (h, b * nq + i, 0)),
                  pl.BlockSpec((1, nq, DH, tq), lambda b, h, i: (h, b, 0, 0)),
                  pl.BlockSpec((1, seqlen, DH), lambda b, h, i: (h, b, 0))],
        out_specs=pl.BlockSpec((1, tq, DH), lambda b, h, i: (h, b * nq + i, 0)),
        out_shape=jax.ShapeDtypeStruct((nh, r, DH), F32),
        scratch_shapes=[pltpu.VMEM((tq, 1), F32), pltpu.VMEM((tq, DH), F32)],
        compiler_params=_params(("parallel", "parallel", "parallel")), name="stickbreak",
    )(qh, kt, vh)


def _dec_body(pt_ref, *refs, npages, page, nq, hb, hc):
    del pt_ref
    n = npages
    pk_b, pv_b, plf = refs[0:n], refs[n:2 * n], refs[2 * n:3 * n]
    pk_c, pv_c = refs[3 * n:4 * n], refs[4 * n:5 * n]
    qb_ref, kb_ref, vb_ref, lf_ref, qc_ref, kc_ref, vc_ref = refs[5 * n:5 * n + 7]
    ob_ref, oc_ref = refs[5 * n + 7:5 * n + 9]
    s_ref = refs[5 * n + 9]

    def q_rows(q, nh):
        w = nh * DH
        qt = jnp.concatenate([q] * nh, axis=0)
        keep = _div_pow2(_iota2((nh * nq, w), 1), DH) == _div_pow2(_iota2((nh * nq, w), 0), nq)
        return jnp.where(keep, qt, 0.0).astype(BF16)

    def pad_rows(x):
        return jnp.concatenate([x, jnp.zeros((page - nq, x.shape[1]), x.dtype)], axis=0)

    def per_head_rows(c, nh):
        return jnp.concatenate([jnp.broadcast_to(c[hh:hh + 1, :], (nq, c.shape[1])) for hh in range(nh)], axis=0)

    def write_heads(o_full, nh, o_ref):
        for hh in range(nh):
            o_ref[hh] = _head_rms(o_full[hh * nq:(hh + 1) * nq, hh * DH:(hh + 1) * DH])

    rows = hb * nq
    qr = q_rows(qb_ref[...], hb)
    q_of_row = _mod_pow2(_iota2((rows, page), 0), nq)
    key = _iota2((rows, page), 1)
    upper = _mask01(_iota2((page, page), 0) <= _iota2((page, page), 1))

    lf_all = jnp.concatenate([plf[p][0, 0] for p in range(n)], axis=0)
    within = _dot_exact_lhs(lf_all, upper)
    carry = jnp.zeros((hb, 1), F32)
    for p in range(n):
        c = within[p * hb:(p + 1) * hb, :] + carry
        carry = c[:, page - 1:page]
        s = jnp.dot(qr, pk_b[p][0, 0].astype(BF16), preferred_element_type=F32) - per_head_rows(c, hb)
        s_ref[:, p * page:(p + 1) * page] = s
    lf_new = lf_ref[...]
    run = [lf_new[0:1, :]]
    for t in range(1, nq):
        run.append(run[-1] + lf_new[t:t + 1, :])
    c_rows = pad_rows(jnp.concatenate(run, axis=0))
    pick = _mask01(_iota2((rows, 128), 1) == _div_pow2(_iota2((rows, 128), 0), nq))
    c_new = None
    for part in _split_bf16(c_rows, 3):
        term = lax.dot_general(pick, part, NT, preferred_element_type=F32)
        c_new = term if c_new is None else c_new + term
    c_new = c_new + per_head_rows(carry, hb)
    k_new = pad_rows(kb_ref[...]).astype(BF16)
    s_new = lax.dot_general(qr, k_new, NT, preferred_element_type=F32) - c_new
    s_ref[:, n * page:(n + 1) * page] = jnp.where(key <= q_of_row, s_new, -jnp.inf)

    s_all = s_ref[...]
    pr = jnp.exp(s_all - jnp.max(s_all, axis=-1, keepdims=True))
    l = jnp.sum(pr, axis=-1, keepdims=True)
    pr = pr.astype(BF16)
    o = jnp.dot(pr[:, n * page:(n + 1) * page], pad_rows(vb_ref[...]).astype(BF16), preferred_element_type=F32)
    for p in range(n):
        o = o + lax.dot_general(pr[:, p * page:(p + 1) * page], pv_b[p][0, 0].astype(BF16), NT,
                                preferred_element_type=F32)
    write_heads(o / l, hb, ob_ref)

    rows = hc * nq
    qr = q_rows(qc_ref[...], hc)
    q_of_row = _mod_pow2(_iota2((rows, page), 0), nq)
    key = _iota2((rows, page), 1)
    later = _mask01(_iota2((page, page), 0) > _iota2((page, page), 1))

    z = lax.dot_general(qr, pad_rows(kc_ref[...]).astype(BF16), NT, preferred_element_type=F32)
    sp = _softplus(z)
    mask = key < q_of_row
    log_keep = jnp.where(mask, -sp, 0.0)
    after = _dot_exact_lhs(log_keep, later, n=2)
    w = jnp.where(mask, jnp.exp(z - sp + after), 0.0)
    o = jnp.dot(w.astype(BF16), pad_rows(vc_ref[...]).astype(BF16), preferred_element_type=F32)
    carry = jnp.sum(log_keep, axis=-1, keepdims=True)
    for p in reversed(range(n)):
        z = jnp.dot(qr, pk_c[p][0, 0].astype(BF16), preferred_element_type=F32)
        sp = _softplus(z)
        after = _dot_exact_lhs(-sp, later, n=2) + carry
        w = jnp.exp(z - sp + after)
        o = o + lax.dot_general(w.astype(BF16), pv_c[p][0, 0].astype(BF16), NT, preferred_element_type=F32)
        carry = carry - jnp.sum(sp, axis=-1, keepdims=True)
    write_heads(o, hc, oc_ref)


def _dec(page_table, layer, cache_fox_k, cache_fox_v, cache_fox_logf, cache_sb_k, cache_sb_v,
         qb, kb, vb, lfr, qc, kc, vc, *, nq):
    nseq, npages = page_table.shape
    page = cache_fox_k.shape[3]
    wb, wc = cache_fox_k.shape[2], cache_sb_k.shape[2]
    hb, hc = wb // DH, wc // DH

    def page_spec(width, p):
        return pl.BlockSpec((1, 1, width, page), lambda b, pt: (layer, pt[b, p], 0, 0))

    in_specs = []
    operands = []
    for cache, width in ((cache_fox_k, wb), (cache_fox_v, wb), (cache_fox_logf, hb),
                         (cache_sb_k, wc), (cache_sb_v, wc)):
        for p in range(npages):
            in_specs.append(page_spec(width, p))
            operands.append(cache)
    new = lambda width: pl.BlockSpec((nq, width), lambda b, pt: (b, 0))
    in_specs += [new(wb), new(wb), new(wb), new(128), new(wc), new(wc), new(wc)]
    operands += [qb, kb, vb, lfr, qc, kc, vc]
    heads = lambda nh: pl.BlockSpec((nh, nq, DH), lambda b, pt: (0, b, 0))
    grid_spec = pltpu.PrefetchScalarGridSpec(
        num_scalar_prefetch=1, grid=(nseq,), in_specs=in_specs,
        out_specs=(heads(hb), heads(hc)),
        scratch_shapes=[pltpu.VMEM((hb * nq, (npages + 1) * page), F32)])
    return pl.pallas_call(
        functools.partial(_dec_body, npages=npages, page=page, nq=nq, hb=hb, hc=hc),
        grid_spec=grid_spec,
        out_shape=(jax.ShapeDtypeStruct((hb, nseq * nq, DH), F32),
                   jax.ShapeDtypeStruct((hc, nseq * nq, DH), F32)),
        compiler_params=_params(("parallel",)), name="paged_decode",
    )(page_table, *operands)


def _out_proj_body(oa_ref, ob_ref, oc_ref, gate_ref, w_ref, x_ref, gpost_ref, y_ref):
    parts = [oa_ref[...]]
    parts += [ob_ref[hh] for hh in range(ob_ref.shape[0])]
    parts += [oc_ref[hh] for hh in range(oc_ref.shape[0])]
    o = jnp.concatenate(parts, axis=-1) * gate_ref[...]
    y = jnp.dot(o.astype(BF16), w_ref[...], preferred_element_type=F32)
    y = y * lax.rsqrt(jnp.mean(y * y, axis=-1, keepdims=True) + EPS) * gpost_ref[...]
    y_ref[...] = x_ref[...] + y


def _out_proj(oa, ob, oc, gate, w, x2d, gpost):
    r, d = x2d.shape
    tm = ROW_TILE
    rows = lambda width: pl.BlockSpec((tm, width), lambda i: (i, 0))
    heads = lambda nh: pl.BlockSpec((nh, tm, DH), lambda i: (0, i, 0))
    full = lambda shape: pl.BlockSpec(shape, lambda i: (0,) * len(shape))
    return pl.pallas_call(
        _out_proj_body, grid=(r // tm,),
        in_specs=[rows(oa.shape[1]), heads(ob.shape[0]), heads(oc.shape[0]), rows(gate.shape[1]),
                  full(w.shape), rows(d), full((1, d))],
        out_specs=rows(d), out_shape=jax.ShapeDtypeStruct((r, d), F32),
        compiler_params=_params(("parallel",)), name="out_proj",
    )(oa, ob, oc, gate, w, x2d, gpost)


def kernel(x_prompt, x_sample, cache_fox_k, cache_fox_v, cache_fox_logf, cache_sb_k, cache_sb_v,
           state_hgrn, page_table, w_in, b_fox, lb_param, w_out, g_pre, g_post, g_out):
    depth, d, n_in = w_in.shape
    mix = w_out.shape[1]
    hb = b_fox.shape[1]
    wa = lb_param.shape[1]
    wb = hb * DH
    wc = mix - wa - wb
    hc = wc // DH
    ha = wa // DKA
    assert n_in == 4 * wa + 4 * wb + hb + 4 * wc
    bp, lp, _ = x_prompt.shape
    bs, ls, _ = x_sample.shape
    n_phys, page = cache_fox_k.shape[1], cache_fox_k.shape[2]
    assert lp % ROW_TILE == 0 and (bs * ls) % ROW_TILE == 0 and HGRN_CHUNK % ls == 0

    o_qb = 4 * wa
    o_fb = o_qb + 4 * wb
    o_qc = o_fb + hb
    w_perm = jnp.concatenate([
        w_in[:, :, 0:3 * wa], w_in[:, :, 3 * wa:4 * wa],
        w_in[:, :, o_qb + 3 * wb:o_qb + 4 * wb], w_in[:, :, o_qc + 3 * wc:o_qc + 4 * wc],
        w_in[:, :, o_qb:o_qb + 3 * wb], w_in[:, :, o_qc:o_qc + 3 * wc]], axis=-1).astype(BF16)
    w_f = w_in[:, :, o_fb:o_fb + hb]
    w_fr = jnp.pad(w_f, ((0, 0), (0, 0), (0, 128 - hb))).astype(BF16)
    w_ft = jnp.pad(jnp.swapaxes(w_f, 1, 2), ((0, 0), (0, 16 - hb), (0, 0))).astype(BF16)
    b_fr = jnp.pad(b_fox, ((0, 0), (0, 128 - hb)))[:, None, :]
    b_ft = jnp.pad(b_fox, ((0, 0), (0, 16 - hb)))[:, :, None]
    w_out_b = w_out.astype(BF16)

    pages_t = [jnp.transpose(a, (0, 1, 3, 4, 2)).reshape(depth, n_phys, -1, page)
               for a in (cache_fox_k, cache_fox_v, cache_sb_k, cache_sb_v)]
    logf_t = jnp.transpose(cache_fox_logf, (0, 1, 3, 2))
    zero_state = jnp.zeros((1, bp, ha, DKA, DVA), F32)

    xp = x_prompt.reshape(bp * lp, d)
    xs = x_sample.reshape(bs * ls, d)
    p_out = [[] for _ in range(6)]
    s_out = [[] for _ in range(6)]
    for l in range(depth):
        proj = functools.partial(_in_proj, gpre=g_pre[l][None, :], w=w_perm[l], gout=g_out[l][None, :],
                                 wa=wa, wb=wb, wc=wc)

        (ua, gate, qbh, kbt, vbh, kbt32, vbt32, qch, kct, vch, kct32, vct32, lft) = proj(
            xp, wf=w_ft[l], bf=b_ft[l], prompt=True, nseq=bp)
        oa, s_fin = _hgrn(ua, lb_param, zero_state, l, 0, nseq_total=bp, seqlen=lp, nseq=1,
                          nchunk=ROW_TILE // HGRN_CHUNK, c=HGRN_CHUNK, ha=ha)
        ob = _fox(qbh, kbt, vbh, _cum(lft, bp, lp), nseq=bp, seqlen=lp)
        oc = _sb(qch, kct, vch, nseq=bp, seqlen=lp)
        xp = _out_proj(oa, ob, oc, gate, w_out_b[l], xp, g_post[l][None, :])
        for lst, a in zip(p_out, (kbt32, vbt32, lft[:hb].reshape(hb, bp, lp), kct32, vct32, s_fin)):
            lst.append(a)

        ua, gate, qb, kb, vb, qc, kc, vc, lfr = proj(xs, wf=w_fr[l], bf=b_fr[l], prompt=False)
        oa, s_fin = _hgrn(ua, lb_param, state_hgrn, l, l, nseq_total=bs, seqlen=ls, nseq=8,
                          nchunk=1, c=ls, ha=ha)
        ob, oc = _dec(page_table, l, pages_t[0], pages_t[1], logf_t, pages_t[2], pages_t[3],
                      qb, kb, vb, lfr, qc, kc, vc, nq=ls)
        xs = _out_proj(oa, ob, oc, gate, w_out_b[l], xs, g_post[l][None, :])
        for lst, a in zip(s_out, (kb.reshape(bs, ls, hb, DH), vb.reshape(bs, ls, hb, DH),
                                  lfr[:, :hb].reshape(bs, ls, hb), kc.reshape(bs, ls, hc, DH),
                                  vc.reshape(bs, ls, hc, DH), s_fin)):
            lst.append(a)

    p_st = [jnp.stack(a, axis=0) for a in p_out]
    s_st = [jnp.stack(a, axis=0) for a in s_out]
    for idx in (0, 1, 3, 4):
        p_st[idx] = jnp.transpose(p_st[idx], (0, 1, 4, 2, 3))
    p_st[2] = jnp.transpose(p_st[2], (0, 2, 3, 1))
    return (xp.reshape(bp, lp, d), xs.reshape(bs, ls, d), *p_st, *s_st)
```

```python
import functools

import jax
import jax.numpy as jnp
from jax import lax
from jax.experimental import pallas as pl
from jax.experimental.pallas import tpu as pltpu

F32 = jnp.float32
BF16 = jnp.bfloat16

EPS = 1e-6
DKA = 128
DVA = 128
DH = 64
HGRN_CHUNK = 32
NEG_BIG = -1e30
SB_DEAD = -104.0

LANES = 128
ROW_TILE = 512
SB_SUB = 256
SB_PREFETCH = 2
DEC_GROUP = 4
N_SPLIT = 3
VMEM_LIMIT = 56 * 1024 * 1024

NT = (((1,), (1,)), ((), ()))
TN = (((0,), (0,)), ((), ()))


def _params(sem):
    return pltpu.CompilerParams(dimension_semantics=sem, vmem_limit_bytes=VMEM_LIMIT)


def _softplus(x):
    return jnp.maximum(x, 0.0) + jnp.log1p(jnp.exp(-jnp.abs(x)))


def _softplus_scores(x):
    return jnp.maximum(x, 0.0) + jnp.log(1.0 + jnp.exp(-jnp.abs(x)))


def _log_sigmoid(x):
    return -_softplus(-x)


def _sigmoid(x):
    return 1.0 / (1.0 + jnp.exp(-x))


def _split_bf16(x, n):
    parts = []
    r = x
    for t in range(n):
        p = r.astype(BF16)
        parts.append(p)
        if t + 1 < n:
            r = r - p.astype(F32)
    return parts


def _dot_exact_lhs(x, m01, n=3):
    acc = None
    for p in _split_bf16(x, n):
        t = jnp.dot(p, m01, preferred_element_type=F32)
        acc = t if acc is None else acc + t
    return acc


def _dot_exact_rhs(m01, x, n=3):
    acc = None
    for p in _split_bf16(x, n):
        t = jnp.dot(m01, p, preferred_element_type=F32)
        acc = t if acc is None else acc + t
    return acc


def _iota2(shape, dim):
    return lax.broadcasted_iota(jnp.int32, shape, dim)


def _mask01(cond):
    return jnp.where(cond, 1.0, 0.0).astype(BF16)


def _div_pow2(x, d):
    assert d & (d - 1) == 0
    return lax.shift_right_logical(x, d.bit_length() - 1)


def _mod_pow2(x, d):
    assert d & (d - 1) == 0
    return lax.bitwise_and(x, d - 1)


def _aligned(start, m):
    return start if isinstance(start, int) else pl.multiple_of(start, m)


def _lane_blocks(ref_blocks):
    return ref_blocks[0] if len(ref_blocks) == 1 else jnp.concatenate(ref_blocks, axis=1)


def _rms_rows_to_cols(o_t):
    o_t = o_t * lax.rsqrt(jnp.mean(o_t * o_t, axis=0, keepdims=True) + EPS)
    padded = jnp.concatenate([o_t, jnp.zeros((LANES - DH, o_t.shape[1]), F32)], axis=0)
    return padded.T[:, :DH]


def _in_proj_body(x_ref, gpre_ref, w_ref, wf_ref, bf_ref, gout_ref, ua_ref, gate_ref, *out_refs,
                  wa, wb, wc, mix, prompt):
    x = x_ref[...]
    tm = x.shape[0]
    h = x * lax.rsqrt(jnp.mean(x * x, axis=-1, keepdims=True) + EPS) * gpre_ref[...]
    hb = h.astype(BF16)

    def proj(lo, hi):
        return jnp.dot(hb, w_ref[:, lo:hi], preferred_element_type=F32)

    ua_ref[...] = proj(0, 3 * wa)
    g = proj(3 * wa, 3 * wa + mix)
    gate_ref[...] = g * _sigmoid(g) * gout_ref[...]

    def attn_group(off, w, refs, pad_k):
        u = proj(off, off + 3 * w)
        q = u[:, :w] * (DH ** -0.5)
        k = u[:, w:2 * w]
        v = u[:, 2 * w:3 * w]
        if not prompt:
            q32_ref, k32_ref, v32_ref = refs
            q32_ref[...] = q
            k32_ref[...] = k
            v32_ref[...] = v
            return
        qt_ref, k_ref, vt_ref, kt32_ref, vt32_ref = refs
        qt = q.T
        kt = k.T
        vt = v.T
        for hh in range(w // DH):
            rows = slice(hh * DH, (hh + 1) * DH)
            kh = k[:, rows]
            if pad_k:
                kh = jnp.concatenate([kh, jnp.zeros((tm, LANES - DH), F32)], axis=1)
            k_ref[hh] = kh.astype(BF16)
            kt32_ref[0, hh] = kt[rows, :]
            vt32_ref[0, hh] = vt[rows, :]
            for jj in range(tm // LANES):
                cols = slice(jj * LANES, (jj + 1) * LANES)
                qt_ref[hh, jj] = qt[rows, cols].astype(BF16)
                vt_ref[hh, jj] = vt[rows, cols].astype(BF16)

    per_group = 5 if prompt else 3
    off_b = 3 * wa + mix
    attn_group(off_b, wb, out_refs[0:per_group], True)
    attn_group(off_b + 3 * wb, wc, out_refs[per_group:2 * per_group], False)

    lf_ref = out_refs[2 * per_group]
    if prompt:
        f = lax.dot_general(wf_ref[...], hb, NT, preferred_element_type=F32)
    else:
        f = jnp.dot(hb, wf_ref[...], preferred_element_type=F32)
    lf_ref[...] = _log_sigmoid(f + bf_ref[...])


def _in_proj(x2d, gpre, w, wf, bf, gout, *, wa, wb, wc, prompt, nseq=1):
    r, d = x2d.shape
    tm = ROW_TILE
    mix = wa + wb + wc
    n = r // tm
    per_seq = n // nseq
    seqlen = r // nseq
    sub = tm // LANES
    full = lambda shape: pl.BlockSpec(shape, lambda i: (0,) * len(shape))
    rows = lambda width: pl.BlockSpec((tm, width), lambda i: (i, 0))
    out_shape = [jax.ShapeDtypeStruct((r, 3 * wa), F32),
                 jax.ShapeDtypeStruct((r, mix), F32)]
    out_specs = [rows(3 * wa), rows(mix)]
    for w_g, kw in ((wb, LANES), (wc, DH)):
        nh = w_g // DH
        if prompt:
            blocked = pl.BlockSpec((nh, sub, DH, LANES), lambda i: (0, i, 0, 0))
            tposed = pl.BlockSpec((1, nh, DH, tm), lambda i: (i // per_seq, 0, 0, i % per_seq))
            out_shape += [jax.ShapeDtypeStruct((nh, r // LANES, DH, LANES), BF16),
                          jax.ShapeDtypeStruct((nh, r, kw), BF16),
                          jax.ShapeDtypeStruct((nh, r // LANES, DH, LANES), BF16),
                          jax.ShapeDtypeStruct((nseq, nh, DH, seqlen), F32),
                          jax.ShapeDtypeStruct((nseq, nh, DH, seqlen), F32)]
            out_specs += [blocked, pl.BlockSpec((nh, tm, kw), lambda i: (0, i, 0)), blocked, tposed, tposed]
        else:
            out_shape += [jax.ShapeDtypeStruct((r, w_g), F32)] * 3
            out_specs += [rows(w_g)] * 3
    if prompt:
        out_shape.append(jax.ShapeDtypeStruct((16, r), F32))
        out_specs.append(pl.BlockSpec((16, tm), lambda i: (0, i)))
    else:
        out_shape.append(jax.ShapeDtypeStruct((r, 128), F32))
        out_specs.append(rows(128))
    in_specs = [rows(d), full((1, d)), full(w.shape), full(wf.shape), full(bf.shape), full((1, mix))]
    return pl.pallas_call(
        functools.partial(_in_proj_body, wa=wa, wb=wb, wc=wc, mix=mix, prompt=prompt),
        grid=(n,), in_specs=in_specs, out_specs=out_specs, out_shape=out_shape,
        compiler_params=_params(("parallel",)), name="in_proj_prompt" if prompt else "in_proj_sample",
    )(x2d, gpre, w, wf, bf, gout)


def _cum_body(lf_ref, k_ref, ka_ref, carry_ref, *, tc, nh):
    @pl.when(pl.program_id(1) == 0)
    def _():
        carry_ref[...] = jnp.zeros_like(carry_ref)

    upper = _mask01(_iota2((tc, tc), 0) <= _iota2((tc, tc), 1))
    c = _dot_exact_lhs(lf_ref[...], upper) + carry_ref[...]
    carry_ref[...] = c[:, tc - 1:tc]
    parts = [p.astype(F32) for p in _split_bf16(c, N_SPLIT)]
    parts = jnp.concatenate(parts + [jnp.zeros((8, tc), F32)], axis=0).astype(BF16)
    r = _iota2((parts.shape[0], LANES), 0)
    lane = _iota2((parts.shape[0], LANES), 1)
    in_extra = jnp.logical_and(lane >= DH, lane < DH + N_SPLIT)
    key_lane = _iota2((tc, LANES), 1)
    for hh in range(nh):
        sel = jnp.where(jnp.logical_and(in_extra, r == (lane - DH) * 8 + hh), -1.0, 0.0).astype(BF16)
        extra = lax.dot_general(parts, sel, TN, preferred_element_type=F32)
        ka_ref[hh] = jnp.where(key_lane < DH, k_ref[hh], extra.astype(BF16))


def _cum(lft, kpad, nseq, seqlen):
    tc = ROW_TILE
    nb = seqlen // tc
    nh = kpad.shape[0]
    return pl.pallas_call(
        functools.partial(_cum_body, tc=tc, nh=nh),
        grid=(nseq, nb),
        in_specs=[pl.BlockSpec((8, tc), lambda b, i: (0, b * nb + i)),
                  pl.BlockSpec((nh, tc, LANES), lambda b, i: (0, b * nb + i, 0))],
        out_specs=pl.BlockSpec((nh, tc, LANES), lambda b, i: (0, b * nb + i, 0)),
        out_shape=jax.ShapeDtypeStruct(kpad.shape, BF16),
        scratch_shapes=[pltpu.VMEM((8, 1), F32)],
        compiler_params=_params(("parallel", "arbitrary")), name="fox_cum",
    )(lft, kpad)


def _hgrn_body(q_ref, f_ref, i_ref, lbp_ref, s0_ref, o_ref, sout_ref,
               st_ref, qin_ref, kout_ref, dec_ref, oin_ref, tri_ref, ds_ref, stb_ref,
               *, layer, nseq, nchunk, c):
    step = pl.program_id(2)
    t = nchunk * c
    rows = nseq * t

    @pl.when(step == 0)
    def _():
        for s in range(nseq):
            st_ref[s] = s0_ref[0, s, 0].T
        ri = _iota2((rows, rows), 0)
        ci = _iota2((rows, rows), 1)
        tri_ref[...] = _mask01(jnp.logical_and(_div_pow2(ri, c) == _div_pow2(ci, c), ci <= ri))

    lbp = lbp_ref[...]
    e = jnp.exp(lbp - jnp.max(lbp, axis=0, keepdims=True))
    sm = e / jnp.sum(e, axis=0, keepdims=True)
    lb = jnp.zeros((1, DKA), F32)
    for r in range(1, layer + 1):
        lb = lb + sm[r:r + 1, :]

    z = f_ref[...]
    qa = q_ref[...]
    q = qa * _sigmoid(qa)
    log_f = jnp.log(lb)
    other = jnp.log1p(-lb) + _log_sigmoid(z)
    mx = jnp.maximum(log_f, other)
    log_f = mx + jnp.log1p(jnp.exp(-jnp.abs(log_f - other)))
    k = (1.0 - lb) * _sigmoid(-z)

    tri = tri_ref[...]
    hi, lo = _split_bf16(log_f, 2)
    b2 = jnp.dot(tri, jnp.concatenate([hi, lo], axis=1), preferred_element_type=F32)
    b = b2[:, :DKA] + b2[:, DKA:]
    b3 = b.reshape(rows // c, c, DKA)
    b_rest = (jnp.broadcast_to(b3[:, c - 1:c, :], b3.shape) - b3).reshape(rows, DKA)
    q_in = q * jnp.exp(b)
    qin_ref[...] = q_in
    kout_ref[...] = k * jnp.exp(b_rest)
    dec_ref[...] = jnp.exp(b + b_rest)
    v_all = i_ref[...].astype(BF16)

    a = lax.dot_general(q_in.astype(BF16), (k * jnp.exp(-b)).astype(BF16), NT, preferred_element_type=F32)
    a = jnp.where(tri > 0, a, 0.0)
    oin_ref[...] = jnp.dot(a.astype(BF16), v_all, preferred_element_type=F32)

    for idx in range(nseq * nchunk):
        lo = idx * c
        ko = kout_ref[lo:lo + c, :].astype(BF16)
        vv = i_ref[lo:lo + c, :].astype(BF16)
        ds_ref[idx] = lax.dot_general(vv, ko, TN, preferred_element_type=F32)
    for s in range(nseq):
        st = st_ref[s]
        for ch in range(nchunk):
            idx = s * nchunk + ch
            stb_ref[idx] = st.astype(BF16)
            st = st * dec_ref[idx * c:idx * c + 1, :] + ds_ref[idx]
        st_ref[s] = st
        sout_ref[s, 0] = st.T
    for idx in range(nseq * nchunk):
        lo = idx * c
        qi = qin_ref[lo:lo + c, :].astype(BF16)
        o = oin_ref[lo:lo + c, :] + lax.dot_general(qi, stb_ref[idx], NT, preferred_element_type=F32)
        o_ref[lo:lo + c, :] = o * lax.rsqrt(jnp.mean(o * o, axis=-1, keepdims=True) + EPS)


def _hgrn(ua, lb_param, s0, layer, s0_layer, *, nseq_total, seqlen, nseq, nchunk, c, ha):
    r = ua.shape[0]
    t = nchunk * c
    nsteps = seqlen // t
    rows = nseq * t
    depth = lb_param.shape[0]
    col = lambda off: pl.BlockSpec((rows, DKA), lambda sb, h, i: (sb * nsteps + i, off + h))
    return pl.pallas_call(
        functools.partial(_hgrn_body, layer=layer, nseq=nseq, nchunk=nchunk, c=c),
        grid=(nseq_total // nseq, ha, nsteps),
        in_specs=[col(0), col(ha), col(2 * ha),
                  pl.BlockSpec((depth, DKA), lambda sb, h, i: (0, h)),
                  pl.BlockSpec((1, nseq, 1, DKA, DVA), lambda sb, h, i: (s0_layer, sb, h, 0, 0))],
        out_specs=(pl.BlockSpec((rows, DVA), lambda sb, h, i: (sb * nsteps + i, h)),
                   pl.BlockSpec((nseq, 1, DKA, DVA), lambda sb, h, i: (sb, h, 0, 0))),
        out_shape=(jax.ShapeDtypeStruct((r, ha * DVA), F32),
                   jax.ShapeDtypeStruct((nseq_total, ha, DKA, DVA), F32)),
        scratch_shapes=[pltpu.VMEM((nseq, DVA, DKA), F32)] + [pltpu.VMEM((rows, DKA), F32)] * 4
                       + [pltpu.VMEM((rows, rows), BF16), pltpu.VMEM((nseq * nchunk, DVA, DKA), F32),
                          pltpu.VMEM((nseq * nchunk, DVA, DKA), BF16)],
        compiler_params=_params(("parallel", "parallel", "arbitrary")), name="hgrn",
    )(ua, ua, ua, lb_param, s0)


def _fox_body(qt_ref, ka_ref, vt_ref, o_ref, m_ref, l_ref, acc_ref, *, tq):
    i = pl.program_id(2)
    sub = tq // LANES
    qt = _lane_blocks([qt_ref[0, s] for s in range(sub)])
    ones = jnp.where(_iota2((LANES - DH, tq), 0) < N_SPLIT, 1.0, 0.0).astype(BF16)
    q_aug = jnp.concatenate([qt, ones], axis=0)

    m_ref[...] = jnp.full_like(m_ref, NEG_BIG)
    l_ref[...] = jnp.zeros_like(l_ref)
    acc_ref[...] = jnp.zeros_like(acc_ref)

    def block(j, nblk, masked):
        tk = nblk * tq
        ka = ka_ref[0, pl.ds(_aligned(j * tq, tq), tk), :]
        vt = _lane_blocks([vt_ref[0, j * sub + s] for s in range(nblk * sub)])
        s = jnp.dot(ka, q_aug, preferred_element_type=F32)
        if masked:
            s = jnp.where(_iota2((tk, tq), 0) <= _iota2((tk, tq), 1), s, -jnp.inf)
        m_old = m_ref[...]
        m_new = jnp.maximum(m_old, jnp.max(s, axis=0, keepdims=True))
        alpha = jnp.exp(m_old - m_new)
        p = jnp.exp(s - m_new)
        l_ref[...] = alpha * l_ref[...] + jnp.sum(p, axis=0, keepdims=True)
        acc_ref[...] = alpha * acc_ref[...] + jnp.dot(vt, p.astype(BF16), preferred_element_type=F32)
        m_ref[...] = m_new

    block(i, 1, True)

    def body(jj, carry):
        block(i - 2 - 2 * jj, 2, False)
        return carry

    lax.fori_loop(0, lax.shift_right_logical(i, 1), body, 0)

    @pl.when(lax.bitwise_and(i, 1) == 1)
    def _():
        block(0, 1, False)

    o_ref[0] = _rms_rows_to_cols(acc_ref[...] / l_ref[...])


def _fox(qt, ka, vt, *, nseq, seqlen):
    nh, r, _ = ka.shape
    tq = ROW_TILE
    nq = seqlen // tq
    sub = tq // LANES
    return pl.pallas_call(
        functools.partial(_fox_body, tq=tq),
        grid=(nseq, nh, nq),
        in_specs=[pl.BlockSpec((1, sub, DH, LANES), lambda b, h, i: (h, b * nq + i, 0, 0)),
                  pl.BlockSpec((1, seqlen, LANES), lambda b, h, i: (h, b, 0)),
                  pl.BlockSpec((1, seqlen // LANES, DH, LANES), lambda b, h, i: (h, b, 0, 0))],
        out_specs=pl.BlockSpec((1, tq, DH), lambda b, h, i: (h, b * nq + i, 0)),
        out_shape=jax.ShapeDtypeStruct((nh, r, DH), F32),
        scratch_shapes=[pltpu.VMEM((1, tq), F32), pltpu.VMEM((1, tq), F32), pltpu.VMEM((DH, tq), F32)],
        compiler_params=_params(("parallel", "parallel", "parallel")), name="fox",
    )(qt, ka, vt)


def _sb_body(qt_ref, k_ref, vt_ref, o_ref, carry_ref, acc_ref, *, tq):
    i = pl.program_id(2)
    ts = SB_SUB
    nsub = tq // ts
    per = ts // LANES
    qt = _lane_blocks([qt_ref[0, s] for s in range(tq // LANES)])
    later = _mask01(_iota2((ts, ts), 1) > _iota2((ts, ts), 0))

    carry_ref[...] = jnp.zeros_like(carry_ref)
    acc_ref[...] = jnp.zeros_like(acc_ref)

    def window(jj0, nw, masked):
        rows = nw * ts
        k = k_ref[0, pl.ds(_aligned(jj0 * ts, ts), rows), :]
        vt = _lane_blocks([vt_ref[0, jj0 * per + s] for s in range(nw * per)])
        z = jnp.dot(k, qt, preferred_element_type=F32)
        sp = _softplus_scores(z)
        if masked:
            valid = (_iota2((rows, tq), 0) - _iota2((rows, tq), 1)) < (i * tq - jj0 * ts)
            log_keep = jnp.where(valid, -sp, 0.0)
        else:
            log_keep = -sp
        pieces = []
        newer = carry_ref[...]
        for s in reversed(range(nw)):
            lk = log_keep[s * ts:(s + 1) * ts, :]
            pieces.append(_dot_exact_rhs(later, lk, n=2) + newer)
            newer = newer + jnp.sum(lk, axis=0, keepdims=True)
        after = pieces[0] if nw == 1 else jnp.concatenate(pieces[::-1], axis=0)
        w = jnp.exp(z - sp + after)
        if masked:
            w = jnp.where(valid, w, 0.0)
        acc_ref[...] += jnp.dot(vt, w.astype(BF16), preferred_element_type=F32)
        carry_ref[...] = newer

    window(jnp.maximum(i * nsub - 1, 0), nsub + 1, True)

    def cond(c):
        jj, live = c
        return jnp.logical_and(jj >= 0, live > SB_DEAD)

    def body(c):
        jj, _ = c
        window(jj, 1, False)
        return jj - 1, jnp.max(carry_ref[...])

    lax.while_loop(cond, body, (i * nsub - 2, jnp.max(carry_ref[...])))
    o_ref[0] = _rms_rows_to_cols(acc_ref[...])


def _sb(qt, k, vt, *, nseq, seqlen):
    nh, r, _ = k.shape
    tq = ROW_TILE
    nq = seqlen // tq
    assert seqlen >= tq + SB_SUB
    return pl.pallas_call(
        functools.partial(_sb_body, tq=tq),
        grid=(nseq, nh, nq),
        in_specs=[pl.BlockSpec((1, tq // LANES, DH, LANES), lambda b, h, i: (h, b * nq + i, 0, 0)),
                  pl.BlockSpec((1, seqlen, DH), lambda b, h, i: (h, b, 0)),
                  pl.BlockSpec((1, seqlen // LANES, DH, LANES), lambda b, h, i: (h, b, 0, 0))],
        out_specs=pl.BlockSpec((1, tq, DH), lambda b, h, i: (h, b * nq + i, 0)),
        out_shape=jax.ShapeDtypeStruct((nh, r, DH), F32),
        scratch_shapes=[pltpu.VMEM((1, tq), F32), pltpu.VMEM((DH, tq), F32)],
        compiler_params=_params(("parallel", "parallel", "parallel")), name="stickbreak",
    )(qt, k, vt)


def _dec_body(pt_ref, fk_hbm, fv_hbm, lf_hbm, ck_hbm, cv_hbm,
              qb_ref, kb_ref, vb_ref, lfn_ref, qc_ref, kc_ref, vc_ref,
              ob_ref, oc_ref,
              fk_buf, fv_buf, lf_buf, ck_buf, cv_buf, xk_buf, xv_buf, o_acc, carry_ref, sem_pre, sem_x,
              *, layer, npages, page, nq, hb, hc, group):
    n = npages
    npre = min(SB_PREFETCH, n)
    step = pl.program_id(0)
    slot = lax.rem(step, 2)

    def prefetch(step, slot):
        out = []
        for g in range(group):
            seq = step * group + g
            for p in range(n):
                pg = pt_ref[seq, p]
                dst = pl.ds(p * page, page)
                out.append(pltpu.make_async_copy(fk_hbm.at[layer, pg], fk_buf.at[slot, g, :, dst], sem_pre.at[slot]))
                out.append(pltpu.make_async_copy(fv_hbm.at[layer, pg], fv_buf.at[slot, g, :, dst], sem_pre.at[slot]))
                out.append(pltpu.make_async_copy(lf_hbm.at[layer, pg], lf_buf.at[slot, g, p], sem_pre.at[slot]))
            for t in range(npre):
                pg = pt_ref[seq, n - npre + t]
                dst = pl.ds(t * page, page)
                out.append(pltpu.make_async_copy(ck_hbm.at[layer, pg], ck_buf.at[slot, g, :, dst], sem_pre.at[slot]))
                out.append(pltpu.make_async_copy(cv_hbm.at[layer, pg], cv_buf.at[slot, g, :, dst], sem_pre.at[slot]))
        return out

    @pl.when(step == 0)
    def _():
        for cp in prefetch(0, 0):
            cp.start()

    @pl.when(step + 1 < pl.num_programs(0))
    def _():
        for cp in prefetch(step + 1, 1 - slot):
            cp.start()

    for cp in prefetch(step, slot):
        cp.wait()

    def q_rows(q, nh):
        w = nh * DH
        qt = jnp.concatenate([q] * nh, axis=0)
        keep = _div_pow2(_iota2((nh * nq, w), 1), DH) == _div_pow2(_iota2((nh * nq, w), 0), nq)
        return jnp.where(keep, qt, 0.0).astype(BF16)

    def new_rows(ref, g):
        return ref[g * nq:(g + 1) * nq, :]

    def new_page(ref, g):
        x = new_rows(ref, g)
        return jnp.concatenate([x, jnp.zeros((page - nq, x.shape[1]), F32)], axis=0).astype(BF16)

    def per_query_rows(c):
        return jnp.concatenate([jnp.broadcast_to(c[i:i + 1, :], (nq, c.shape[1])) for i in range(c.shape[0])], axis=0)

    def stack(per_seq):
        return per_seq[0] if group == 1 else jnp.concatenate(per_seq, axis=0)

    def write_heads(o_full, nh, o_ref, g):
        for hh in range(nh):
            blk = o_full[hh * nq:(hh + 1) * nq, hh * DH:(hh + 1) * DH]
            o_ref[hh, g * nq:(g + 1) * nq, :] = blk * lax.rsqrt(jnp.mean(blk * blk, axis=-1, keepdims=True) + EPS)

    assert hb == hc
    rows = hb * nq
    upper = _mask01(_iota2((page, page), 0) <= _iota2((page, page), 1))
    later = _mask01(_iota2((page, page), 0) > _iota2((page, page), 1))
    key = _iota2((group * rows, page), 1)
    q_of_row = _mod_pow2(_iota2((group * rows, page), 0), nq)

    qr = [q_rows(new_rows(qb_ref, g), hb) for g in range(group)]
    lf_all = jnp.concatenate([lf_buf[slot, g, p] for p in range(n) for g in range(group)], axis=0)
    within = _dot_exact_lhs(lf_all, upper)
    carry = jnp.zeros((group * hb, 1), F32)
    c_pages = []
    for p in range(n):
        c = within[p * group * hb:(p + 1) * group * hb, :] + carry
        carry = c[:, page - 1:page]
        c_pages.append(per_query_rows(c))
    pick = _mask01(_iota2((rows, 128), 1) == _div_pow2(_iota2((rows, 128), 0), nq))
    c_new = []
    for g in range(group):
        lf_new = new_rows(lfn_ref, g)
        run = [lf_new[0:1, :]]
        for t in range(1, nq):
            run.append(run[-1] + lf_new[t:t + 1, :])
        c_rows = jnp.concatenate(run + [jnp.zeros((page - nq, lf_new.shape[1]), F32)], axis=0)
        acc = None
        for part in _split_bf16(c_rows, 3):
            term = lax.dot_general(pick, part, NT, preferred_element_type=F32)
            acc = term if acc is None else acc + term
        c_new.append(acc)
    c_new = stack(c_new) + per_query_rows(carry)
    s_past = stack([jnp.dot(qr[g], fk_buf[slot, g].astype(BF16), preferred_element_type=F32) for g in range(group)])
    s_past = s_past - jnp.concatenate(c_pages, axis=1)
    s_new = stack([lax.dot_general(qr[g], new_page(kb_ref, g), NT, preferred_element_type=F32)
                   for g in range(group)]) - c_new
    s_new = jnp.where(key <= q_of_row, s_new, -jnp.inf)
    m = jnp.maximum(jnp.max(s_past, axis=-1, keepdims=True), jnp.max(s_new, axis=-1, keepdims=True))
    p_past = jnp.exp(s_past - m)
    p_new = jnp.exp(s_new - m)
    inv_l = 1.0 / (jnp.sum(p_past, axis=-1, keepdims=True) + jnp.sum(p_new, axis=-1, keepdims=True))
    p_past = p_past.astype(BF16)
    p_new = p_new.astype(BF16)
    for g in range(group):
        sl = slice(g * rows, (g + 1) * rows)
        o = lax.dot_general(p_past[sl], fv_buf[slot, g].astype(BF16), NT, preferred_element_type=F32)
        o = o + jnp.dot(p_new[sl], new_page(vb_ref, g), preferred_element_type=F32)
        write_heads(o * inv_l[sl], hb, ob_ref, g)

    def sb_pages(z_pages, v_of_page, mask_newest, carry_in):
        out = None
        newer = carry_in
        r = z_pages[0].shape[0]
        for idx in reversed(range(len(z_pages))):
            z = z_pages[idx]
            sp = _softplus_scores(z)
            masked = mask_newest and idx == len(z_pages) - 1
            log_keep = jnp.where(key[:r] < q_of_row[:r], -sp, 0.0) if masked else -sp
            after = _dot_exact_lhs(log_keep, later, n=2) + newer
            w = jnp.exp(z - sp + after)
            if masked:
                w = jnp.where(key[:r] < q_of_row[:r], w, 0.0)
            term = v_of_page(idx, w.astype(BF16))
            out = term if out is None else out + term
            newer = newer + jnp.sum(log_keep, axis=-1, keepdims=True)
        return out, newer

    qr = [q_rows(new_rows(qc_ref, g), hc) for g in range(group)]
    cv = [cv_buf[slot, g].astype(BF16) for g in range(group)]
    v_new = [new_page(vc_ref, g) for g in range(group)]
    z_pre = stack([jnp.dot(qr[g], ck_buf[slot, g].astype(BF16), preferred_element_type=F32) for g in range(group)])
    z_pages = [z_pre[:, t * page:(t + 1) * page] for t in range(npre)]
    z_pages.append(stack([lax.dot_general(qr[g], new_page(kc_ref, g), NT, preferred_element_type=F32)
                          for g in range(group)]))

    def v_pre(idx, w):
        outs = []
        for g in range(group):
            wg = w[g * rows:(g + 1) * rows]
            if idx == npre:
                outs.append(jnp.dot(wg, v_new[g], preferred_element_type=F32))
            else:
                outs.append(lax.dot_general(wg, cv[g][:, idx * page:(idx + 1) * page], NT,
                                            preferred_element_type=F32))
        return stack(outs)

    o, carry = sb_pages(z_pages, v_pre, True, jnp.zeros((group * rows, 1), F32))
    o_acc[...] = o
    carry_ref[...] = carry

    for g in range(group):
        sl = slice(g * rows, (g + 1) * rows)
        seq = step * group + g

        def cond(c):
            p, live = c
            return jnp.logical_and(p >= 0, live > SB_DEAD)

        def body(c, g=g, sl=sl, seq=seq):
            p, _ = c
            pg = pt_ref[seq, p]
            cp_k = pltpu.make_async_copy(ck_hbm.at[layer, pg], xk_buf, sem_x.at[0])
            cp_v = pltpu.make_async_copy(cv_hbm.at[layer, pg], xv_buf, sem_x.at[1])
            cp_k.start()
            cp_v.start()
            cp_k.wait()
            cp_v.wait()
            z = jnp.dot(qr[g], xk_buf[...].astype(BF16), preferred_element_type=F32)
            xv = xv_buf[...].astype(BF16)
            o_p, newer = sb_pages([z], lambda idx, w: lax.dot_general(w, xv, NT, preferred_element_type=F32),
                                  False, carry_ref[sl, :])
            o_acc[sl, :] += o_p
            carry_ref[sl, :] = newer
            return p - 1, jnp.max(newer)

        lax.while_loop(cond, body, (n - npre - 1, jnp.max(carry_ref[sl, :])))
        write_heads(o_acc[sl, :], hc, oc_ref, g)


def _dec(page_table, layer, cache_fox_k, cache_fox_v, cache_fox_logf, cache_sb_k, cache_sb_v,
         qb, kb, vb, lfr, qc, kc, vc, *, nq):
    nseq, npages = page_table.shape
    page = cache_fox_k.shape[3]
    wb, wc = cache_fox_k.shape[2], cache_sb_k.shape[2]
    hb, hc = wb // DH, wc // DH
    npre = min(SB_PREFETCH, npages)
    group = DEC_GROUP
    assert nseq % group == 0
    hbm = pl.BlockSpec(memory_space=pl.ANY)
    new = lambda width: pl.BlockSpec((group * nq, width), lambda b, pt: (b, 0))
    heads = lambda nh: pl.BlockSpec((nh, group * nq, DH), lambda b, pt: (0, b, 0))
    grid_spec = pltpu.PrefetchScalarGridSpec(
        num_scalar_prefetch=1, grid=(nseq // group,),
        in_specs=[hbm] * 5 + [new(wb), new(wb), new(wb), new(128), new(wc), new(wc), new(wc)],
        out_specs=(heads(hb), heads(hc)),
        scratch_shapes=[pltpu.VMEM((2, group, wb, npages * page), F32),
                        pltpu.VMEM((2, group, wb, npages * page), F32),
                        pltpu.VMEM((2, group, npages, hb, page), F32),
                        pltpu.VMEM((2, group, wc, npre * page), F32), pltpu.VMEM((2, group, wc, npre * page), F32),
                        pltpu.VMEM((wc, page), F32), pltpu.VMEM((wc, page), F32),
                        pltpu.VMEM((group * hc * nq, wc), F32), pltpu.VMEM((group * hc * nq, 1), F32),
                        pltpu.SemaphoreType.DMA((2,)), pltpu.SemaphoreType.DMA((2,))])
    return pl.pallas_call(
        functools.partial(_dec_body, layer=layer, npages=npages, page=page, nq=nq, hb=hb, hc=hc, group=group),
        grid_spec=grid_spec,
        out_shape=(jax.ShapeDtypeStruct((hb, nseq * nq, DH), F32),
                   jax.ShapeDtypeStruct((hc, nseq * nq, DH), F32)),
        compiler_params=_params(("arbitrary",)), name="paged_decode",
    )(page_table, cache_fox_k, cache_fox_v, cache_fox_logf, cache_sb_k, cache_sb_v, qb, kb, vb, lfr, qc, kc, vc)


def _out_proj_body(oa_ref, ob_ref, oc_ref, gate_ref, w_ref, x_ref, gpost_ref, y_ref):
    parts = [oa_ref[...]]
    parts += [ob_ref[hh] for hh in range(ob_ref.shape[0])]
    parts += [oc_ref[hh] for hh in range(oc_ref.shape[0])]
    o = jnp.concatenate(parts, axis=-1) * gate_ref[...]
    y = jnp.dot(o.astype(BF16), w_ref[...], preferred_element_type=F32)
    y = y * lax.rsqrt(jnp.mean(y * y, axis=-1, keepdims=True) + EPS) * gpost_ref[...]
    y_ref[...] = x_ref[...] + y


def _out_proj(oa, ob, oc, gate, w, x2d, gpost):
    r, d = x2d.shape
    tm = ROW_TILE
    rows = lambda width: pl.BlockSpec((tm, width), lambda i: (i, 0))
    heads = lambda nh: pl.BlockSpec((nh, tm, DH), lambda i: (0, i, 0))
    full = lambda shape: pl.BlockSpec(shape, lambda i: (0,) * len(shape))
    return pl.pallas_call(
        _out_proj_body, grid=(r // tm,),
        in_specs=[rows(oa.shape[1]), heads(ob.shape[0]), heads(oc.shape[0]), rows(gate.shape[1]),
                  full(w.shape), rows(d), full((1, d))],
        out_specs=rows(d), out_shape=jax.ShapeDtypeStruct((r, d), F32),
        compiler_params=_params(("parallel",)), name="out_proj",
    )(oa, ob, oc, gate, w, x2d, gpost)


def kernel(x_prompt, x_sample, cache_fox_k, cache_fox_v, cache_fox_logf, cache_sb_k, cache_sb_v,
           state_hgrn, page_table, w_in, b_fox, lb_param, w_out, g_pre, g_post, g_out):
    depth, d, n_in = w_in.shape
    mix = w_out.shape[1]
    hb = b_fox.shape[1]
    wa = lb_param.shape[1]
    wb = hb * DH
    wc = mix - wa - wb
    hc = wc // DH
    ha = wa // DKA
    assert n_in == 4 * wa + 4 * wb + hb + 4 * wc
    bp, lp, _ = x_prompt.shape
    bs, ls, _ = x_sample.shape
    n_phys, page = cache_fox_k.shape[1], cache_fox_k.shape[2]
    assert lp % ROW_TILE == 0 and (bs * ls) % ROW_TILE == 0 and HGRN_CHUNK % ls == 0

    o_qb = 4 * wa
    o_fb = o_qb + 4 * wb
    o_qc = o_fb + hb
    w_perm = jnp.concatenate([
        w_in[:, :, 0:3 * wa], w_in[:, :, 3 * wa:4 * wa],
        w_in[:, :, o_qb + 3 * wb:o_qb + 4 * wb], w_in[:, :, o_qc + 3 * wc:o_qc + 4 * wc],
        w_in[:, :, o_qb:o_qb + 3 * wb], w_in[:, :, o_qc:o_qc + 3 * wc]], axis=-1).astype(BF16)
    w_f = w_in[:, :, o_fb:o_fb + hb]
    w_fr = jnp.pad(w_f, ((0, 0), (0, 0), (0, 128 - hb))).astype(BF16)
    w_ft = jnp.pad(jnp.swapaxes(w_f, 1, 2), ((0, 0), (0, 16 - hb), (0, 0))).astype(BF16)
    b_fr = jnp.pad(b_fox, ((0, 0), (0, 128 - hb)))[:, None, :]
    b_ft = jnp.pad(b_fox, ((0, 0), (0, 16 - hb)))[:, :, None]
    w_out_b = w_out.astype(BF16)

    pages_t = [jnp.transpose(a, (0, 1, 3, 4, 2)).reshape(depth, n_phys, -1, page)
               for a in (cache_fox_k, cache_fox_v, cache_sb_k, cache_sb_v)]
    logf_t = jnp.transpose(cache_fox_logf, (0, 1, 3, 2))
    zero_state = jnp.zeros((1, bp, ha, DKA, DVA), F32)

    xp = x_prompt.reshape(bp * lp, d)
    xs = x_sample.reshape(bs * ls, d)
    p_out = [[] for _ in range(6)]
    s_out = [[] for _ in range(6)]
    for l in range(depth):
        proj = functools.partial(_in_proj, gpre=g_pre[l][None, :], w=w_perm[l], gout=g_out[l][None, :],
                                 wa=wa, wb=wb, wc=wc)

        (ua, gate, qbt, kbp, vbt, kbt32, vbt32, qct, kcr, vct, kct32, vct32, lft) = proj(
            xp, wf=w_ft[l], bf=b_ft[l], prompt=True, nseq=bp)
        oa, s_fin = _hgrn(ua, lb_param, zero_state, l, 0, nseq_total=bp, seqlen=lp, nseq=1,
                          nchunk=ROW_TILE // HGRN_CHUNK, c=HGRN_CHUNK, ha=ha)
        ob = _fox(qbt, _cum(lft, kbp, bp, lp), vbt, nseq=bp, seqlen=lp)
        oc = _sb(qct, kcr, vct, nseq=bp, seqlen=lp)
        xp = _out_proj(oa, ob, oc, gate, w_out_b[l], xp, g_post[l][None, :])
        for lst, a in zip(p_out, (kbt32, vbt32, lft[:hb].reshape(hb, bp, lp), kct32, vct32, s_fin)):
            lst.append(a)

        ua, gate, qb, kb, vb, qc, kc, vc, lfr = proj(xs, wf=w_fr[l], bf=b_fr[l], prompt=False)
        oa, s_fin = _hgrn(ua, lb_param, state_hgrn, l, l, nseq_total=bs, seqlen=ls, nseq=8,
                          nchunk=1, c=ls, ha=ha)
        ob, oc = _dec(page_table, l, pages_t[0], pages_t[1], logf_t, pages_t[2], pages_t[3],
                      qb, kb, vb, lfr, qc, kc, vc, nq=ls)
        xs = _out_proj(oa, ob, oc, gate, w_out_b[l], xs, g_post[l][None, :])
        for lst, a in zip(s_out, (kb.reshape(bs, ls, hb, DH), vb.reshape(bs, ls, hb, DH),
                                  lfr[:, :hb].reshape(bs, ls, hb), kc.reshape(bs, ls, hc, DH),
                                  vc.reshape(bs, ls, hc, DH), s_fin)):
            lst.append(a)

    p_st = [jnp.stack(a, axis=0) for a in p_out]
    s_st = [jnp.stack(a, axis=0) for a in s_out]
    for idx in (0, 1, 3, 4):
        p_st[idx] = jnp.transpose(p_st[idx], (0, 1, 4, 2, 3))
    p_st[2] = jnp.transpose(p_st[2], (0, 2, 3, 1))
    return (xp.reshape(bp, lp, d), xs.reshape(bs, ls, d), *p_st, *s_st)
```

```python
import functools

import jax
import jax.numpy as jnp
from jax import lax
from jax.experimental import pallas as pl
from jax.experimental.pallas import tpu as pltpu

F32 = jnp.float32
BF16 = jnp.bfloat16

EPS = 1e-6
DKA = 128
DVA = 128
DH = 64
HGRN_CHUNK = 32
NEG_BIG = -1e30
SB_DEAD = -104.0
FOX_DEAD = -106.0

LANES = 128
ROW_TILE = 512
SB_SUB = 256
SB_PREFETCH = 2
DEC_GROUP = 4
N_SPLIT = 3
VMEM_LIMIT = 56 * 1024 * 1024

NT = (((1,), (1,)), ((), ()))
TN = (((0,), (0,)), ((), ()))


def _params(sem):
    return pltpu.CompilerParams(dimension_semantics=sem, vmem_limit_bytes=VMEM_LIMIT)


def _softplus(x):
    return jnp.maximum(x, 0.0) + jnp.log1p(jnp.exp(-jnp.abs(x)))


def _softplus_scores(x):
    return jnp.maximum(x, 0.0) + jnp.log(1.0 + jnp.exp(-jnp.abs(x)))


def _log_sigmoid(x):
    return -_softplus(-x)


def _sigmoid(x):
    return 1.0 / (1.0 + jnp.exp(-x))


def _split_bf16(x, n):
    parts = []
    r = x
    for t in range(n):
        p = r.astype(BF16)
        parts.append(p)
        if t + 1 < n:
            r = r - p.astype(F32)
    return parts


def _dot_exact_lhs(x, m01, n=3):
    acc = None
    for p in _split_bf16(x, n):
        t = jnp.dot(p, m01, preferred_element_type=F32)
        acc = t if acc is None else acc + t
    return acc


def _dot_exact_rhs(m01, x, n=3):
    acc = None
    for p in _split_bf16(x, n):
        t = jnp.dot(m01, p, preferred_element_type=F32)
        acc = t if acc is None else acc + t
    return acc


def _iota2(shape, dim):
    return lax.broadcasted_iota(jnp.int32, shape, dim)


def _mask01(cond):
    return jnp.where(cond, 1.0, 0.0).astype(BF16)


def _div_pow2(x, d):
    assert d & (d - 1) == 0
    return lax.shift_right_logical(x, d.bit_length() - 1)


def _mod_pow2(x, d):
    assert d & (d - 1) == 0
    return lax.bitwise_and(x, d - 1)


def _aligned(start, m):
    return start if isinstance(start, int) else pl.multiple_of(start, m)


def _lane_blocks(ref_blocks):
    return ref_blocks[0] if len(ref_blocks) == 1 else jnp.concatenate(ref_blocks, axis=1)


def _rms_rows_to_cols(o_t):
    o_t = o_t * lax.rsqrt(jnp.mean(o_t * o_t, axis=0, keepdims=True) + EPS)
    padded = jnp.concatenate([o_t, jnp.zeros((LANES - DH, o_t.shape[1]), F32)], axis=0)
    return padded.T[:, :DH]


def _in_proj_body(x_ref, gpre_ref, w_ref, wf_ref, bf_ref, gout_ref, ua_ref, gate_ref, *out_refs,
                  wa, wb, wc, mix, prompt):
    x = x_ref[...]
    tm = x.shape[0]
    h = x * lax.rsqrt(jnp.mean(x * x, axis=-1, keepdims=True) + EPS) * gpre_ref[...]
    hb = h.astype(BF16)

    def proj(lo, hi):
        return jnp.dot(hb, w_ref[:, lo:hi], preferred_element_type=F32)

    ua_ref[...] = proj(0, 3 * wa)
    g = proj(3 * wa, 3 * wa + mix)
    gate_ref[...] = g * _sigmoid(g) * gout_ref[...]

    def attn_group(off, w, refs, pad_k):
        u = proj(off, off + 3 * w)
        q = u[:, :w] * (DH ** -0.5)
        k = u[:, w:2 * w]
        v = u[:, 2 * w:3 * w]
        if not prompt:
            q32_ref, k32_ref, v32_ref = refs
            q32_ref[...] = q
            k32_ref[...] = k
            v32_ref[...] = v
            return
        qt_ref, k_ref, vt_ref, kt32_ref, vt32_ref = refs
        qt = q.T
        kt = k.T
        vt = v.T
        for hh in range(w // DH):
            rows = slice(hh * DH, (hh + 1) * DH)
            kh = k[:, rows]
            if pad_k:
                kh = jnp.concatenate([kh, jnp.zeros((tm, LANES - DH), F32)], axis=1)
            k_ref[hh] = kh.astype(BF16)
            kt32_ref[0, hh] = kt[rows, :]
            vt32_ref[0, hh] = vt[rows, :]
            for jj in range(tm // LANES):
                cols = slice(jj * LANES, (jj + 1) * LANES)
                qt_ref[hh, jj] = qt[rows, cols].astype(BF16)
                vt_ref[hh, jj] = vt[rows, cols].astype(BF16)

    per_group = 5 if prompt else 3
    off_b = 3 * wa + mix
    attn_group(off_b, wb, out_refs[0:per_group], True)
    attn_group(off_b + 3 * wb, wc, out_refs[per_group:2 * per_group], False)

    lf_ref = out_refs[2 * per_group]
    if prompt:
        f = lax.dot_general(wf_ref[...], hb, NT, preferred_element_type=F32)
    else:
        f = jnp.dot(hb, wf_ref[...], preferred_element_type=F32)
    lf_ref[...] = _log_sigmoid(f + bf_ref[...])


def _in_proj(x2d, gpre, w, wf, bf, gout, *, wa, wb, wc, prompt, nseq=1):
    r, d = x2d.shape
    tm = ROW_TILE
    mix = wa + wb + wc
    n = r // tm
    per_seq = n // nseq
    seqlen = r // nseq
    sub = tm // LANES
    full = lambda shape: pl.BlockSpec(shape, lambda i: (0,) * len(shape))
    rows = lambda width: pl.BlockSpec((tm, width), lambda i: (i, 0))
    out_shape = [jax.ShapeDtypeStruct((r, 3 * wa), F32),
                 jax.ShapeDtypeStruct((r, mix), F32)]
    out_specs = [rows(3 * wa), rows(mix)]
    for w_g, kw in ((wb, LANES), (wc, DH)):
        nh = w_g // DH
        if prompt:
            blocked = pl.BlockSpec((nh, sub, DH, LANES), lambda i: (0, i, 0, 0))
            tposed = pl.BlockSpec((1, nh, DH, tm), lambda i: (i // per_seq, 0, 0, i % per_seq))
            out_shape += [jax.ShapeDtypeStruct((nh, r // LANES, DH, LANES), BF16),
                          jax.ShapeDtypeStruct((nh, r, kw), BF16),
                          jax.ShapeDtypeStruct((nh, r // LANES, DH, LANES), BF16),
                          jax.ShapeDtypeStruct((nseq, nh, DH, seqlen), F32),
                          jax.ShapeDtypeStruct((nseq, nh, DH, seqlen), F32)]
            out_specs += [blocked, pl.BlockSpec((nh, tm, kw), lambda i: (0, i, 0)), blocked, tposed, tposed]
        else:
            out_shape += [jax.ShapeDtypeStruct((r, w_g), F32)] * 3
            out_specs += [rows(w_g)] * 3
    if prompt:
        out_shape.append(jax.ShapeDtypeStruct((16, r), F32))
        out_specs.append(pl.BlockSpec((16, tm), lambda i: (0, i)))
    else:
        out_shape.append(jax.ShapeDtypeStruct((r, 128), F32))
        out_specs.append(rows(128))
    in_specs = [rows(d), full((1, d)), full(w.shape), full(wf.shape), full(bf.shape), full((1, mix))]
    return pl.pallas_call(
        functools.partial(_in_proj_body, wa=wa, wb=wb, wc=wc, mix=mix, prompt=prompt),
        grid=(n,), in_specs=in_specs, out_specs=out_specs, out_shape=out_shape,
        compiler_params=_params(("parallel",)), name="in_proj_prompt" if prompt else "in_proj_sample",
    )(x2d, gpre, w, wf, bf, gout)


def _cum_body(lf_ref, k_ref, ka_ref, kn_ref, cl_ref, carry_ref, knmax_ref, *, tc, nh):
    @pl.when(pl.program_id(1) == 0)
    def _():
        carry_ref[...] = jnp.zeros_like(carry_ref)
        knmax_ref[...] = jnp.zeros_like(knmax_ref)

    upper = _mask01(_iota2((tc, tc), 0) <= _iota2((tc, tc), 1))
    c = _dot_exact_lhs(lf_ref[...], upper) + carry_ref[...]
    carry_ref[...] = c[:, tc - 1:tc]
    parts = [p.astype(F32) for p in _split_bf16(c, N_SPLIT)]
    parts = jnp.concatenate(parts + [jnp.zeros((8, tc), F32)], axis=0).astype(BF16)
    r = _iota2((parts.shape[0], LANES), 0)
    lane = _iota2((parts.shape[0], LANES), 1)
    in_extra = jnp.logical_and(lane >= DH, lane < DH + N_SPLIT)
    key_lane = _iota2((tc, LANES), 1)
    norms = []
    for hh in range(nh):
        sel = jnp.where(jnp.logical_and(in_extra, r == (lane - DH) * 8 + hh), -1.0, 0.0).astype(BF16)
        extra = lax.dot_general(parts, sel, TN, preferred_element_type=F32)
        kh = k_ref[hh]
        ka_ref[hh] = jnp.where(key_lane < DH, kh, extra.astype(BF16))
        kf = kh.astype(F32)
        norms.append(jnp.sqrt(jnp.max(jnp.sum(kf * kf, axis=1, keepdims=True), axis=0, keepdims=True)))
    knmax = jnp.maximum(knmax_ref[...], jnp.concatenate(norms + [jnp.zeros((8 - nh, 1), F32)], axis=0))
    knmax_ref[...] = knmax
    kn_ref[0] = jnp.broadcast_to(knmax, (8, tc))
    cl_ref[0] = jnp.broadcast_to(c[:, tc - 1:tc], (8, tc))


def _cum(lft, kpad, nseq, seqlen):
    tc = ROW_TILE
    nb = seqlen // tc
    nh = kpad.shape[0]
    return pl.pallas_call(
        functools.partial(_cum_body, tc=tc, nh=nh),
        grid=(nseq, nb),
        in_specs=[pl.BlockSpec((8, tc), lambda b, i: (0, b * nb + i)),
                  pl.BlockSpec((nh, tc, LANES), lambda b, i: (0, b * nb + i, 0))],
        out_specs=(pl.BlockSpec((nh, tc, LANES), lambda b, i: (0, b * nb + i, 0)),
                   pl.BlockSpec((1, 8, tc), lambda b, i: (b * nb + i, 0, 0)),
                   pl.BlockSpec((1, 8, tc), lambda b, i: (b * nb + i, 0, 0))),
        out_shape=(jax.ShapeDtypeStruct(kpad.shape, BF16),
                   jax.ShapeDtypeStruct((nseq * nb, 8, tc), F32),
                   jax.ShapeDtypeStruct((nseq * nb, 8, tc), F32)),
        scratch_shapes=[pltpu.VMEM((8, 1), F32), pltpu.VMEM((8, 1), F32)],
        compiler_params=_params(("parallel", "arbitrary")), name="fox_cum",
    )(lft, kpad)


def _hgrn_body(q_ref, f_ref, i_ref, lbp_ref, s0_ref, o_ref, sout_ref,
               st_ref, qin_ref, kout_ref, dec_ref, oin_ref, tri_ref, ds_ref, stb_ref,
               *, layer, nseq, nchunk, c):
    step = pl.program_id(2)
    t = nchunk * c
    rows = nseq * t

    @pl.when(step == 0)
    def _():
        for s in range(nseq):
            st_ref[s] = s0_ref[0, s, 0].T
        ri = _iota2((rows, rows), 0)
        ci = _iota2((rows, rows), 1)
        tri_ref[...] = _mask01(jnp.logical_and(_div_pow2(ri, c) == _div_pow2(ci, c), ci <= ri))

    lbp = lbp_ref[...]
    e = jnp.exp(lbp - jnp.max(lbp, axis=0, keepdims=True))
    sm = e / jnp.sum(e, axis=0, keepdims=True)
    lb = jnp.zeros((1, DKA), F32)
    for r in range(1, layer + 1):
        lb = lb + sm[r:r + 1, :]

    z = f_ref[...]
    qa = q_ref[...]
    q = qa * _sigmoid(qa)
    log_f = jnp.log(lb)
    other = jnp.log1p(-lb) + _log_sigmoid(z)
    mx = jnp.maximum(log_f, other)
    log_f = mx + jnp.log1p(jnp.exp(-jnp.abs(log_f - other)))
    k = (1.0 - lb) * _sigmoid(-z)

    tri = tri_ref[...]
    hi, lo = _split_bf16(log_f, 2)
    b2 = jnp.dot(tri, jnp.concatenate([hi, lo], axis=1), preferred_element_type=F32)
    b = b2[:, :DKA] + b2[:, DKA:]
    b3 = b.reshape(rows // c, c, DKA)
    b_rest = (jnp.broadcast_to(b3[:, c - 1:c, :], b3.shape) - b3).reshape(rows, DKA)
    q_in = q * jnp.exp(b)
    qin_ref[...] = q_in
    kout_ref[...] = k * jnp.exp(b_rest)
    dec_ref[...] = jnp.exp(b + b_rest)
    v_all = i_ref[...].astype(BF16)

    a = lax.dot_general(q_in.astype(BF16), (k * jnp.exp(-b)).astype(BF16), NT, preferred_element_type=F32)
    a = jnp.where(tri > 0, a, 0.0)
    oin_ref[...] = jnp.dot(a.astype(BF16), v_all, preferred_element_type=F32)

    for idx in range(nseq * nchunk):
        lo = idx * c
        ko = kout_ref[lo:lo + c, :].astype(BF16)
        vv = i_ref[lo:lo + c, :].astype(BF16)
        ds_ref[idx] = lax.dot_general(vv, ko, TN, preferred_element_type=F32)
    for s in range(nseq):
        st = st_ref[s]
        for ch in range(nchunk):
            idx = s * nchunk + ch
            stb_ref[idx] = st.astype(BF16)
            st = st * dec_ref[idx * c:idx * c + 1, :] + ds_ref[idx]
        st_ref[s] = st
        sout_ref[s, 0] = st.T
    for idx in range(nseq * nchunk):
        lo = idx * c
        qi = qin_ref[lo:lo + c, :].astype(BF16)
        o = oin_ref[lo:lo + c, :] + lax.dot_general(qi, stb_ref[idx], NT, preferred_element_type=F32)
        o_ref[lo:lo + c, :] = o * lax.rsqrt(jnp.mean(o * o, axis=-1, keepdims=True) + EPS)


def _hgrn(ua, lb_param, s0, layer, s0_layer, *, nseq_total, seqlen, nseq, nchunk, c, ha):
    r = ua.shape[0]
    t = nchunk * c
    nsteps = seqlen // t
    rows = nseq * t
    depth = lb_param.shape[0]
    col = lambda off: pl.BlockSpec((rows, DKA), lambda sb, h, i: (sb * nsteps + i, off + h))
    return pl.pallas_call(
        functools.partial(_hgrn_body, layer=layer, nseq=nseq, nchunk=nchunk, c=c),
        grid=(nseq_total // nseq, ha, nsteps),
        in_specs=[col(0), col(ha), col(2 * ha),
                  pl.BlockSpec((depth, DKA), lambda sb, h, i: (0, h)),
                  pl.BlockSpec((1, nseq, 1, DKA, DVA), lambda sb, h, i: (s0_layer, sb, h, 0, 0))],
        out_specs=(pl.BlockSpec((rows, DVA), lambda sb, h, i: (sb * nsteps + i, h)),
                   pl.BlockSpec((nseq, 1, DKA, DVA), lambda sb, h, i: (sb, h, 0, 0))),
        out_shape=(jax.ShapeDtypeStruct((r, ha * DVA), F32),
                   jax.ShapeDtypeStruct((nseq_total, ha, DKA, DVA), F32)),
        scratch_shapes=[pltpu.VMEM((nseq, DVA, DKA), F32)] + [pltpu.VMEM((rows, DKA), F32)] * 4
                       + [pltpu.VMEM((rows, rows), BF16), pltpu.VMEM((nseq * nchunk, DVA, DKA), F32),
                          pltpu.VMEM((nseq * nchunk, DVA, DKA), BF16)],
        compiler_params=_params(("parallel", "parallel", "arbitrary")), name="hgrn",
    )(ua, ua, ua, lb_param, s0)


def _fox_body(qt_ref, ka_ref, vt_ref, kn_ref, cl_ref, o_ref, m_ref, l_ref, acc_ref, *, tq):
    h = pl.program_id(1)
    i = pl.program_id(2)
    sub = tq // LANES
    qt = _lane_blocks([qt_ref[0, s] for s in range(sub)])
    ones = jnp.where(_iota2((LANES - DH, tq), 0) < N_SPLIT, 1.0, 0.0).astype(BF16)
    q_aug = jnp.concatenate([qt, ones], axis=0)
    qf = qt.astype(F32)
    qn = jnp.sqrt(jnp.sum(qf * qf, axis=0, keepdims=True))

    def live(j):
        bound = qn * kn_ref[j, pl.ds(h, 1), :] - cl_ref[j, pl.ds(h, 1), :] - m_ref[...]
        return jnp.max(bound)

    m_ref[...] = jnp.full_like(m_ref, NEG_BIG)
    l_ref[...] = jnp.zeros_like(l_ref)
    acc_ref[...] = jnp.zeros_like(acc_ref)

    def scores(j, nblk):
        ka = ka_ref[0, pl.ds(_aligned(j * tq, tq), nblk * tq), :]
        return jnp.dot(ka, q_aug, preferred_element_type=F32)

    def consume(s, j, nblk, masked):
        tk = nblk * tq
        vt = _lane_blocks([vt_ref[0, j * sub + t] for t in range(nblk * sub)])
        if masked:
            s = jnp.where(_iota2((tk, tq), 0) <= _iota2((tk, tq), 1), s, -jnp.inf)
        m_old = m_ref[...]
        m_new = jnp.maximum(m_old, jnp.max(s, axis=0, keepdims=True))
        alpha = jnp.exp(m_old - m_new)
        p = jnp.exp(s - m_new)
        l_ref[...] = alpha * l_ref[...] + jnp.sum(p, axis=0, keepdims=True)
        acc_ref[...] = alpha * acc_ref[...] + jnp.dot(vt, p.astype(BF16), preferred_element_type=F32)
        m_ref[...] = m_new

    consume(scores(i, 1), i, 1, True)

    nquad = lax.shift_right_logical(i, 2)

    def cond(c):
        kk, alive = c
        return jnp.logical_and(kk < nquad, alive > FOX_DEAD)

    def body(c):
        kk, _ = c
        ja = i - 2 - 4 * kk
        jb = ja - 2
        sa = scores(ja, 2)
        sb = scores(jb, 2)
        consume(sa, ja, 2, False)
        consume(sb, jb, 2, False)
        return kk + 1, live(jnp.maximum(jb - 1, 0))

    lax.while_loop(cond, body, (0, live(jnp.maximum(i - 1, 0))))
    rest = lax.bitwise_and(i, 3)
    rest_alive = live(jnp.maximum(rest - 1, 0)) > FOX_DEAD

    @pl.when(jnp.logical_and(rest >= 2, rest_alive))
    def _():
        j0 = lax.bitwise_and(rest, 1)
        consume(scores(j0, 2), j0, 2, False)

    @pl.when(jnp.logical_and(lax.bitwise_and(rest, 1) == 1, rest_alive))
    def _():
        consume(scores(0, 1), 0, 1, False)

    o_ref[0] = _rms_rows_to_cols(acc_ref[...] / l_ref[...])


def _fox(qt, ka, vt, kn, cl, *, nseq, seqlen):
    nh, r, _ = ka.shape
    tq = ROW_TILE
    nq = seqlen // tq
    sub = tq // LANES
    return pl.pallas_call(
        functools.partial(_fox_body, tq=tq),
        grid=(nseq, nh, nq),
        in_specs=[pl.BlockSpec((1, sub, DH, LANES), lambda b, h, i: (h, b * nq + i, 0, 0)),
                  pl.BlockSpec((1, seqlen, LANES), lambda b, h, i: (h, b, 0)),
                  pl.BlockSpec((1, seqlen // LANES, DH, LANES), lambda b, h, i: (h, b, 0, 0)),
                  pl.BlockSpec((nq, 8, tq), lambda b, h, i: (b, 0, 0)),
                  pl.BlockSpec((nq, 8, tq), lambda b, h, i: (b, 0, 0))],
        out_specs=pl.BlockSpec((1, tq, DH), lambda b, h, i: (h, b * nq + i, 0)),
        out_shape=jax.ShapeDtypeStruct((nh, r, DH), F32),
        scratch_shapes=[pltpu.VMEM((1, tq), F32), pltpu.VMEM((1, tq), F32), pltpu.VMEM((DH, tq), F32)],
        compiler_params=_params(("parallel", "parallel", "parallel")), name="fox",
    )(qt, ka, vt, kn, cl)


def _sb_body(qt_ref, k_ref, vt_ref, o_ref, carry_ref, acc_ref, *, tq):
    i = pl.program_id(2)
    ts = SB_SUB
    nsub = tq // ts
    per = ts // LANES
    qt = _lane_blocks([qt_ref[0, s] for s in range(tq // LANES)])
    later = _mask01(_iota2((ts, ts), 1) > _iota2((ts, ts), 0))
    later2 = jnp.concatenate([later, later], axis=1)

    carry_ref[...] = jnp.zeros_like(carry_ref)
    acc_ref[...] = jnp.zeros_like(acc_ref)

    def window(jj0, nw, masked):
        rows = nw * ts
        k = k_ref[0, pl.ds(_aligned(jj0 * ts, ts), rows), :]
        vt = _lane_blocks([vt_ref[0, jj0 * per + s] for s in range(nw * per)])
        z = jnp.dot(k, qt, preferred_element_type=F32)
        if masked:
            z = jnp.where((_iota2((rows, tq), 0) - _iota2((rows, tq), 1)) < (i * tq - jj0 * ts), z, NEG_BIG)
        sp = _softplus_scores(z)
        pieces = []
        newer = carry_ref[...]
        for s in reversed(range(nw)):
            sp_s = sp[s * ts:(s + 1) * ts, :]
            hi, lo = _split_bf16(sp_s, 2)
            behind = jnp.dot(later2, jnp.concatenate([hi, lo], axis=0), preferred_element_type=F32)
            pieces.append(behind + newer)
            newer = newer + behind[0:1, :] + sp_s[0:1, :]
        behind = pieces[0] if nw == 1 else jnp.concatenate(pieces[::-1], axis=0)
        w = jnp.exp(z - sp - behind)
        acc_ref[...] += jnp.dot(vt, w.astype(BF16), preferred_element_type=F32)
        carry_ref[...] = newer

    window(jnp.maximum(i * nsub - 1, 0), nsub + 1, True)

    def cond(c):
        jj, live = c
        return jnp.logical_and(jj >= 0, live < -SB_DEAD)

    def body(c):
        jj, _ = c
        window(jj, 1, False)
        return jj - 1, jnp.min(carry_ref[...])

    lax.while_loop(cond, body, (i * nsub - 2, jnp.min(carry_ref[...])))
    o_ref[0] = _rms_rows_to_cols(acc_ref[...])


def _sb(qt, k, vt, *, nseq, seqlen):
    nh, r, _ = k.shape
    tq = ROW_TILE
    nq = seqlen // tq
    assert seqlen >= tq + SB_SUB
    return pl.pallas_call(
        functools.partial(_sb_body, tq=tq),
        grid=(nseq, nh, nq),
        in_specs=[pl.BlockSpec((1, tq // LANES, DH, LANES), lambda b, h, i: (h, b * nq + i, 0, 0)),
                  pl.BlockSpec((1, seqlen, DH), lambda b, h, i: (h, b, 0)),
                  pl.BlockSpec((1, seqlen // LANES, DH, LANES), lambda b, h, i: (h, b, 0, 0))],
        out_specs=pl.BlockSpec((1, tq, DH), lambda b, h, i: (h, b * nq + i, 0)),
        out_shape=jax.ShapeDtypeStruct((nh, r, DH), F32),
        scratch_shapes=[pltpu.VMEM((1, tq), F32), pltpu.VMEM((DH, tq), F32)],
        compiler_params=_params(("parallel", "parallel", "parallel")), name="stickbreak",
    )(qt, k, vt)


def _dec_body(pt_ref, fk_hbm, fv_hbm, lf_hbm, ck_hbm, cv_hbm,
              qb_ref, kb_ref, vb_ref, lfn_ref, qc_ref, kc_ref, vc_ref,
              ob_ref, oc_ref,
              fk_buf, fv_buf, lf_buf, ck_buf, cv_buf, xk_buf, xv_buf, o_acc, carry_ref, sem_pre, sem_x,
              *, layer, npages, page, nq, hb, hc, group):
    n = npages
    npre = min(SB_PREFETCH, n)
    step = pl.program_id(0)
    slot = lax.rem(step, 2)

    def prefetch(step, slot):
        out = []
        for g in range(group):
            seq = step * group + g
            for p in range(n):
                pg = pt_ref[seq, p]
                dst = pl.ds(p * page, page)
                out.append(pltpu.make_async_copy(fk_hbm.at[layer, pg], fk_buf.at[slot, g, :, dst], sem_pre.at[slot]))
                out.append(pltpu.make_async_copy(fv_hbm.at[layer, pg], fv_buf.at[slot, g, :, dst], sem_pre.at[slot]))
                out.append(pltpu.make_async_copy(lf_hbm.at[layer, pg], lf_buf.at[slot, g, p], sem_pre.at[slot]))
            for t in range(npre):
                pg = pt_ref[seq, n - npre + t]
                dst = pl.ds(t * page, page)
                out.append(pltpu.make_async_copy(ck_hbm.at[layer, pg], ck_buf.at[slot, g, :, dst], sem_pre.at[slot]))
                out.append(pltpu.make_async_copy(cv_hbm.at[layer, pg], cv_buf.at[slot, g, :, dst], sem_pre.at[slot]))
        return out

    @pl.when(step == 0)
    def _():
        for cp in prefetch(0, 0):
            cp.start()

    @pl.when(step + 1 < pl.num_programs(0))
    def _():
        for cp in prefetch(step + 1, 1 - slot):
            cp.start()

    for cp in prefetch(step, slot):
        cp.wait()

    def q_rows(q, nh):
        w = nh * DH
        qt = jnp.concatenate([q] * nh, axis=0)
        keep = _div_pow2(_iota2((nh * nq, w), 1), DH) == _div_pow2(_iota2((nh * nq, w), 0), nq)
        return jnp.where(keep, qt, 0.0).astype(BF16)

    def new_rows(ref, g):
        return ref[g * nq:(g + 1) * nq, :]

    def new_page(ref, g):
        x = new_rows(ref, g)
        return jnp.concatenate([x, jnp.zeros((page - nq, x.shape[1]), F32)], axis=0).astype(BF16)

    def per_query_rows(c):
        return jnp.concatenate([jnp.broadcast_to(c[i:i + 1, :], (nq, c.shape[1])) for i in range(c.shape[0])], axis=0)

    def stack(per_seq):
        return per_seq[0] if group == 1 else jnp.concatenate(per_seq, axis=0)

    def write_heads(o_full, nh, o_ref, g):
        for hh in range(nh):
            blk = o_full[hh * nq:(hh + 1) * nq, hh * DH:(hh + 1) * DH]
            o_ref[hh, g * nq:(g + 1) * nq, :] = blk * lax.rsqrt(jnp.mean(blk * blk, axis=-1, keepdims=True) + EPS)

    assert hb == hc
    rows = hb * nq
    upper = _mask01(_iota2((page, page), 0) <= _iota2((page, page), 1))
    later = _mask01(_iota2((page, page), 0) > _iota2((page, page), 1))
    key = _iota2((group * rows, page), 1)
    q_of_row = _mod_pow2(_iota2((group * rows, page), 0), nq)

    qr = [q_rows(new_rows(qb_ref, g), hb) for g in range(group)]
    lf_all = jnp.concatenate([lf_buf[slot, g, p] for p in range(n) for g in range(group)], axis=0)
    within = _dot_exact_lhs(lf_all, upper)
    carry = jnp.zeros((group * hb, 1), F32)
    c_pages = []
    for p in range(n):
        c = within[p * group * hb:(p + 1) * group * hb, :] + carry
        carry = c[:, page - 1:page]
        c_pages.append(per_query_rows(c))
    pick = _mask01(_iota2((rows, 128), 1) == _div_pow2(_iota2((rows, 128), 0), nq))
    c_new = []
    for g in range(group):
        lf_new = new_rows(lfn_ref, g)
        run = [lf_new[0:1, :]]
        for t in range(1, nq):
            run.append(run[-1] + lf_new[t:t + 1, :])
        c_rows = jnp.concatenate(run + [jnp.zeros((page - nq, lf_new.shape[1]), F32)], axis=0)
        acc = None
        for part in _split_bf16(c_rows, 3):
            term = lax.dot_general(pick, part, NT, preferred_element_type=F32)
            acc = term if acc is None else acc + term
        c_new.append(acc)
    c_new = stack(c_new) + per_query_rows(carry)
    s_past = stack([jnp.dot(qr[g], fk_buf[slot, g].astype(BF16), preferred_element_type=F32) for g in range(group)])
    s_past = s_past - jnp.concatenate(c_pages, axis=1)
    s_new = stack([lax.dot_general(qr[g], new_page(kb_ref, g), NT, preferred_element_type=F32)
                   for g in range(group)]) - c_new
    s_new = jnp.where(key <= q_of_row, s_new, -jnp.inf)
    m = jnp.maximum(jnp.max(s_past, axis=-1, keepdims=True), jnp.max(s_new, axis=-1, keepdims=True))
    p_past = jnp.exp(s_past - m)
    p_new = jnp.exp(s_new - m)
    inv_l = 1.0 / (jnp.sum(p_past, axis=-1, keepdims=True) + jnp.sum(p_new, axis=-1, keepdims=True))
    p_past = p_past.astype(BF16)
    p_new = p_new.astype(BF16)
    for g in range(group):
        sl = slice(g * rows, (g + 1) * rows)
        o = lax.dot_general(p_past[sl], fv_buf[slot, g].astype(BF16), NT, preferred_element_type=F32)
        o = o + jnp.dot(p_new[sl], new_page(vb_ref, g), preferred_element_type=F32)
        write_heads(o * inv_l[sl], hb, ob_ref, g)

    def sb_pages(z_pages, v_of_page, mask_newest, carry_in):
        out = None
        newer = carry_in
        r = z_pages[0].shape[0]
        for idx in reversed(range(len(z_pages))):
            z = z_pages[idx]
            sp = _softplus_scores(z)
            masked = mask_newest and idx == len(z_pages) - 1
            log_keep = jnp.where(key[:r] < q_of_row[:r], -sp, 0.0) if masked else -sp
            after = _dot_exact_lhs(log_keep, later, n=2) + newer
            w = jnp.exp(z - sp + after)
            if masked:
                w = jnp.where(key[:r] < q_of_row[:r], w, 0.0)
            term = v_of_page(idx, w.astype(BF16))
            out = term if out is None else out + term
            newer = newer + jnp.sum(log_keep, axis=-1, keepdims=True)
        return out, newer

    qr = [q_rows(new_rows(qc_ref, g), hc) for g in range(group)]
    cv = [cv_buf[slot, g].astype(BF16) for g in range(group)]
    v_new = [new_page(vc_ref, g) for g in range(group)]
    z_pre = stack([jnp.dot(qr[g], ck_buf[slot, g].astype(BF16), preferred_element_type=F32) for g in range(group)])
    z_pages = [z_pre[:, t * page:(t + 1) * page] for t in range(npre)]
    z_pages.append(stack([lax.dot_general(qr[g], new_page(kc_ref, g), NT, preferred_element_type=F32)
                          for g in range(group)]))

    def v_pre(idx, w):
        outs = []
        for g in range(group):
            wg = w[g * rows:(g + 1) * rows]
            if idx == npre:
                outs.append(jnp.dot(wg, v_new[g], preferred_element_type=F32))
            else:
                outs.append(lax.dot_general(wg, cv[g][:, idx * page:(idx + 1) * page], NT,
                                            preferred_element_type=F32))
        return stack(outs)

    o, carry = sb_pages(z_pages, v_pre, True, jnp.zeros((group * rows, 1), F32))
    o_acc[...] = o
    carry_ref[...] = carry

    for g in range(group):
        sl = slice(g * rows, (g + 1) * rows)
        seq = step * group + g

        def cond(c):
            p, live = c
            return jnp.logical_and(p >= 0, live > SB_DEAD)

        def body(c, g=g, sl=sl, seq=seq):
            p, _ = c
            pg = pt_ref[seq, p]
            cp_k = pltpu.make_async_copy(ck_hbm.at[layer, pg], xk_buf, sem_x.at[0])
            cp_v = pltpu.make_async_copy(cv_hbm.at[layer, pg], xv_buf, sem_x.at[1])
            cp_k.start()
            cp_v.start()
            cp_k.wait()
            cp_v.wait()
            z = jnp.dot(qr[g], xk_buf[...].astype(BF16), preferred_element_type=F32)
            xv = xv_buf[...].astype(BF16)
            o_p, newer = sb_pages([z], lambda idx, w: lax.dot_general(w, xv, NT, preferred_element_type=F32),
                                  False, carry_ref[sl, :])
            o_acc[sl, :] += o_p
            carry_ref[sl, :] = newer
            return p - 1, jnp.max(newer)

        lax.while_loop(cond, body, (n - npre - 1, jnp.max(carry_ref[sl, :])))
        write_heads(o_acc[sl, :], hc, oc_ref, g)


def _dec(page_table, layer, cache_fox_k, cache_fox_v, cache_fox_logf, cache_sb_k, cache_sb_v,
         qb, kb, vb, lfr, qc, kc, vc, *, nq):
    nseq, npages = page_table.shape
    page = cache_fox_k.shape[3]
    wb, wc = cache_fox_k.shape[2], cache_sb_k.shape[2]
    hb, hc = wb // DH, wc // DH
    npre = min(SB_PREFETCH, npages)
    group = DEC_GROUP
    assert nseq % group == 0
    hbm = pl.BlockSpec(memory_space=pl.ANY)
    new = lambda width: pl.BlockSpec((group * nq, width), lambda b, pt: (b, 0))
    heads = lambda nh: pl.BlockSpec((nh, group * nq, DH), lambda b, pt: (0, b, 0))
    grid_spec = pltpu.PrefetchScalarGridSpec(
        num_scalar_prefetch=1, grid=(nseq // group,),
        in_specs=[hbm] * 5 + [new(wb), new(wb), new(wb), new(128), new(wc), new(wc), new(wc)],
        out_specs=(heads(hb), heads(hc)),
        scratch_shapes=[pltpu.VMEM((2, group, wb, npages * page), F32),
                        pltpu.VMEM((2, group, wb, npages * page), F32),
                        pltpu.VMEM((2, group, npages, hb, page), F32),
                        pltpu.VMEM((2, group, wc, npre * page), F32), pltpu.VMEM((2, group, wc, npre * page), F32),
                        pltpu.VMEM((wc, page), F32), pltpu.VMEM((wc, page), F32),
                        pltpu.VMEM((group * hc * nq, wc), F32), pltpu.VMEM((group * hc * nq, 1), F32),
                        pltpu.SemaphoreType.DMA((2,)), pltpu.SemaphoreType.DMA((2,))])
    return pl.pallas_call(
        functools.partial(_dec_body, layer=layer, npages=npages, page=page, nq=nq, hb=hb, hc=hc, group=group),
        grid_spec=grid_spec,
        out_shape=(jax.ShapeDtypeStruct((hb, nseq * nq, DH), F32),
                   jax.ShapeDtypeStruct((hc, nseq * nq, DH), F32)),
        compiler_params=_params(("arbitrary",)), name="paged_decode",
    )(page_table, cache_fox_k, cache_fox_v, cache_fox_logf, cache_sb_k, cache_sb_v, qb, kb, vb, lfr, qc, kc, vc)


def _out_proj_body(oa_ref, ob_ref, oc_ref, gate_ref, w_ref, x_ref, gpost_ref, y_ref):
    parts = [oa_ref[...]]
    parts += [ob_ref[hh] for hh in range(ob_ref.shape[0])]
    parts += [oc_ref[hh] for hh in range(oc_ref.shape[0])]
    o = jnp.concatenate(parts, axis=-1) * gate_ref[...]
    y = jnp.dot(o.astype(BF16), w_ref[...], preferred_element_type=F32)
    y = y * lax.rsqrt(jnp.mean(y * y, axis=-1, keepdims=True) + EPS) * gpost_ref[...]
    y_ref[...] = x_ref[...] + y


def _out_proj(oa, ob, oc, gate, w, x2d, gpost):
    r, d = x2d.shape
    tm = ROW_TILE
    rows = lambda width: pl.BlockSpec((tm, width), lambda i: (i, 0))
    heads = lambda nh: pl.BlockSpec((nh, tm, DH), lambda i: (0, i, 0))
    full = lambda shape: pl.BlockSpec(shape, lambda i: (0,) * len(shape))
    return pl.pallas_call(
        _out_proj_body, grid=(r // tm,),
        in_specs=[rows(oa.shape[1]), heads(ob.shape[0]), heads(oc.shape[0]), rows(gate.shape[1]),
                  full(w.shape), rows(d), full((1, d))],
        out_specs=rows(d), out_shape=jax.ShapeDtypeStruct((r, d), F32),
        compiler_params=_params(("parallel",)), name="out_proj",
    )(oa, ob, oc, gate, w, x2d, gpost)


def kernel(x_prompt, x_sample, cache_fox_k, cache_fox_v, cache_fox_logf, cache_sb_k, cache_sb_v,
           state_hgrn, page_table, w_in, b_fox, lb_param, w_out, g_pre, g_post, g_out):
    depth, d, n_in = w_in.shape
    mix = w_out.shape[1]
    hb = b_fox.shape[1]
    wa = lb_param.shape[1]
    wb = hb * DH
    wc = mix - wa - wb
    hc = wc // DH
    ha = wa // DKA
    assert n_in == 4 * wa + 4 * wb + hb + 4 * wc
    bp, lp, _ = x_prompt.shape
    bs, ls, _ = x_sample.shape
    n_phys, page = cache_fox_k.shape[1], cache_fox_k.shape[2]
    assert lp % ROW_TILE == 0 and (bs * ls) % ROW_TILE == 0 and HGRN_CHUNK % ls == 0

    o_qb = 4 * wa
    o_fb = o_qb + 4 * wb
    o_qc = o_fb + hb
    w_perm = jnp.concatenate([
        w_in[:, :, 0:3 * wa], w_in[:, :, 3 * wa:4 * wa],
        w_in[:, :, o_qb + 3 * wb:o_qb + 4 * wb], w_in[:, :, o_qc + 3 * wc:o_qc + 4 * wc],
        w_in[:, :, o_qb:o_qb + 3 * wb], w_in[:, :, o_qc:o_qc + 3 * wc]], axis=-1).astype(BF16)
    w_f = w_in[:, :, o_fb:o_fb + hb]
    w_fr = jnp.pad(w_f, ((0, 0), (0, 0), (0, 128 - hb))).astype(BF16)
    w_ft = jnp.pad(jnp.swapaxes(w_f, 1, 2), ((0, 0), (0, 16 - hb), (0, 0))).astype(BF16)
    b_fr = jnp.pad(b_fox, ((0, 0), (0, 128 - hb)))[:, None, :]
    b_ft = jnp.pad(b_fox, ((0, 0), (0, 16 - hb)))[:, :, None]
    w_out_b = w_out.astype(BF16)

    pages_t = [jnp.transpose(a, (0, 1, 3, 4, 2)).reshape(depth, n_phys, -1, page)
               for a in (cache_fox_k, cache_fox_v, cache_sb_k, cache_sb_v)]
    logf_t = jnp.transpose(cache_fox_logf, (0, 1, 3, 2))
    zero_state = jnp.zeros((1, bp, ha, DKA, DVA), F32)

    xp = x_prompt.reshape(bp * lp, d)
    xs = x_sample.reshape(bs * ls, d)
    p_out = [[] for _ in range(6)]
    s_out = [[] for _ in range(6)]
    for l in range(depth):
        proj = functools.partial(_in_proj, gpre=g_pre[l][None, :], w=w_perm[l], gout=g_out[l][None, :],
                                 wa=wa, wb=wb, wc=wc)

        (ua, gate, qbt, kbp, vbt, kbt32, vbt32, qct, kcr, vct, kct32, vct32, lft) = proj(
            xp, wf=w_ft[l], bf=b_ft[l], prompt=True, nseq=bp)
        oa, s_fin = _hgrn(ua, lb_param, zero_state, l, 0, nseq_total=bp, seqlen=lp, nseq=1,
                          nchunk=ROW_TILE // HGRN_CHUNK, c=HGRN_CHUNK, ha=ha)
        kba, kn, cl = _cum(lft, kbp, bp, lp)
        ob = _fox(qbt, kba, vbt, kn, cl, nseq=bp, seqlen=lp)
        oc = _sb(qct, kcr, vct, nseq=bp, seqlen=lp)
        xp = _out_proj(oa, ob, oc, gate, w_out_b[l], xp, g_post[l][None, :])
        for lst, a in zip(p_out, (kbt32, vbt32, lft[:hb].reshape(hb, bp, lp), kct32, vct32, s_fin)):
            lst.append(a)

        ua, gate, qb, kb, vb, qc, kc, vc, lfr = proj(xs, wf=w_fr[l], bf=b_fr[l], prompt=False)
        oa, s_fin = _hgrn(ua, lb_param, state_hgrn, l, l, nseq_total=bs, seqlen=ls, nseq=8,
                          nchunk=1, c=ls, ha=ha)
        ob, oc = _dec(page_table, l, pages_t[0], pages_t[1], logf_t, pages_t[2], pages_t[3],
                      qb, kb, vb, lfr, qc, kc, vc, nq=ls)
        xs = _out_proj(oa, ob, oc, gate, w_out_b[l], xs, g_post[l][None, :])
        for lst, a in zip(s_out, (kb.reshape(bs, ls, hb, DH), vb.reshape(bs, ls, hb, DH),
                                  lfr[:, :hb].reshape(bs, ls, hb), kc.reshape(bs, ls, hc, DH),
                                  vc.reshape(bs, ls, hc, DH), s_fin)):
            lst.append(a)

    p_st = [jnp.stack(a, axis=0) for a in p_out]
    s_st = [jnp.stack(a, axis=0) for a in s_out]
    for idx in (0, 1, 3, 4):
        p_st[idx] = jnp.transpose(p_st[idx], (0, 1, 4, 2, 3))
    p_st[2] = jnp.transpose(p_st[2], (0, 2, 3, 1))
    return (xp.reshape(bp, lp, d), xs.reshape(bs, ls, d), *p_st, *s_st)
```

```python
import functools

import jax
import jax.numpy as jnp
from jax import lax
from jax.experimental import pallas as pl
from jax.experimental.pallas import tpu as pltpu

F32 = jnp.float32
BF16 = jnp.bfloat16

EPS = 1e-6
DKA = 128
DVA = 128
DH = 64
HGRN_CHUNK = 32
HGRN_HEADS = 2
NEG_BIG = -1e30
SB_DEAD = -104.0
FOX_DEAD = -106.0

LANES = 128
ROW_TILE = 512
SB_SUB = 256
SB_PREFETCH = 2
DEC_GROUP = 4
N_SPLIT = 3
VMEM_LIMIT = 56 * 1024 * 1024

NT = (((1,), (1,)), ((), ()))
TN = (((0,), (0,)), ((), ()))


def _params(sem):
    return pltpu.CompilerParams(dimension_semantics=sem, vmem_limit_bytes=VMEM_LIMIT)


def _softplus(x):
    return jnp.maximum(x, 0.0) + jnp.log1p(jnp.exp(-jnp.abs(x)))


def _softplus_scores(x):
    return jnp.maximum(x, 0.0) + jnp.log(1.0 + jnp.exp(-jnp.abs(x)))


def _log_sigmoid(x):
    return -_softplus(-x)


def _sigmoid(x):
    return 1.0 / (1.0 + jnp.exp(-x))


def _split_bf16(x, n):
    parts = []
    r = x
    for t in range(n):
        p = r.astype(BF16)
        parts.append(p)
        if t + 1 < n:
            r = r - p.astype(F32)
    return parts


def _dot_exact_lhs(x, m01, n=3):
    acc = None
    for p in _split_bf16(x, n):
        t = jnp.dot(p, m01, preferred_element_type=F32)
        acc = t if acc is None else acc + t
    return acc


def _dot_exact_rhs(m01, x, n=3):
    acc = None
    for p in _split_bf16(x, n):
        t = jnp.dot(m01, p, preferred_element_type=F32)
        acc = t if acc is None else acc + t
    return acc


def _iota2(shape, dim):
    return lax.broadcasted_iota(jnp.int32, shape, dim)


def _mask01(cond):
    return jnp.where(cond, 1.0, 0.0).astype(BF16)


def _div_pow2(x, d):
    assert d & (d - 1) == 0
    return lax.shift_right_logical(x, d.bit_length() - 1)


def _mod_pow2(x, d):
    assert d & (d - 1) == 0
    return lax.bitwise_and(x, d - 1)


def _aligned(start, m):
    return start if isinstance(start, int) else pl.multiple_of(start, m)


def _lane_blocks(ref_blocks):
    return ref_blocks[0] if len(ref_blocks) == 1 else jnp.concatenate(ref_blocks, axis=1)


def _rms_rows_to_cols(o_t):
    o_t = o_t * lax.rsqrt(jnp.mean(o_t * o_t, axis=0, keepdims=True) + EPS)
    padded = jnp.concatenate([o_t, jnp.zeros((LANES - DH, o_t.shape[1]), F32)], axis=0)
    return padded.T[:, :DH]


def _in_proj_body(x_ref, gpre_ref, w_ref, wf_ref, bf_ref, gout_ref, ua_ref, gate_ref, *out_refs,
                  wa, wb, wc, mix, prompt):
    x = x_ref[...]
    tm = x.shape[0]
    h = x * lax.rsqrt(jnp.mean(x * x, axis=-1, keepdims=True) + EPS) * gpre_ref[...]
    hb = h.astype(BF16)

    def proj(lo, hi):
        return jnp.dot(hb, w_ref[:, lo:hi], preferred_element_type=F32)

    ua_ref[...] = proj(0, 3 * wa)
    g = proj(3 * wa, 3 * wa + mix)
    gate_ref[...] = g * _sigmoid(g) * gout_ref[...]

    def attn_group(off, w, refs, pad_k):
        u = proj(off, off + 3 * w)
        q = u[:, :w] * (DH ** -0.5)
        k = u[:, w:2 * w]
        v = u[:, 2 * w:3 * w]
        if not prompt:
            q32_ref, k32_ref, v32_ref = refs
            q32_ref[...] = q
            k32_ref[...] = k
            v32_ref[...] = v
            return
        qt_ref, k_ref, vt_ref, kt32_ref, vt32_ref = refs
        qt = q.T
        kt = k.T
        vt = v.T
        for hh in range(w // DH):
            rows = slice(hh * DH, (hh + 1) * DH)
            kh = k[:, rows]
            if pad_k:
                kh = jnp.concatenate([kh, jnp.zeros((tm, LANES - DH), F32)], axis=1)
            k_ref[hh] = kh.astype(BF16)
            kt32_ref[0, hh] = kt[rows, :]
            vt32_ref[0, hh] = vt[rows, :]
            for jj in range(tm // LANES):
                cols = slice(jj * LANES, (jj + 1) * LANES)
                qt_ref[hh, jj] = qt[rows, cols].astype(BF16)
                vt_ref[hh, jj] = vt[rows, cols].astype(BF16)

    per_group = 5 if prompt else 3
    off_b = 3 * wa + mix
    attn_group(off_b, wb, out_refs[0:per_group], True)
    attn_group(off_b + 3 * wb, wc, out_refs[per_group:2 * per_group], False)

    lf_ref = out_refs[2 * per_group]
    if prompt:
        f = lax.dot_general(wf_ref[...], hb, NT, preferred_element_type=F32)
    else:
        f = jnp.dot(hb, wf_ref[...], preferred_element_type=F32)
    lf_ref[...] = _log_sigmoid(f + bf_ref[...])


def _in_proj(x2d, gpre, w, wf, bf, gout, *, wa, wb, wc, prompt, nseq=1):
    r, d = x2d.shape
    tm = ROW_TILE
    mix = wa + wb + wc
    n = r // tm
    per_seq = n // nseq
    seqlen = r // nseq
    sub = tm // LANES
    full = lambda shape: pl.BlockSpec(shape, lambda i: (0,) * len(shape))
    rows = lambda width: pl.BlockSpec((tm, width), lambda i: (i, 0))
    out_shape = [jax.ShapeDtypeStruct((r, 3 * wa), F32),
                 jax.ShapeDtypeStruct((r, mix), F32)]
    out_specs = [rows(3 * wa), rows(mix)]
    for w_g, kw in ((wb, LANES), (wc, DH)):
        nh = w_g // DH
        if prompt:
            blocked = pl.BlockSpec((nh, sub, DH, LANES), lambda i: (0, i, 0, 0))
            tposed = pl.BlockSpec((1, nh, DH, tm), lambda i: (i // per_seq, 0, 0, i % per_seq))
            out_shape += [jax.ShapeDtypeStruct((nh, r // LANES, DH, LANES), BF16),
                          jax.ShapeDtypeStruct((nh, r, kw), BF16),
                          jax.ShapeDtypeStruct((nh, r // LANES, DH, LANES), BF16),
                          jax.ShapeDtypeStruct((nseq, nh, DH, seqlen), F32),
                          jax.ShapeDtypeStruct((nseq, nh, DH, seqlen), F32)]
            out_specs += [blocked, pl.BlockSpec((nh, tm, kw), lambda i: (0, i, 0)), blocked, tposed, tposed]
        else:
            out_shape += [jax.ShapeDtypeStruct((r, w_g), F32)] * 3
            out_specs += [rows(w_g)] * 3
    if prompt:
        out_shape.append(jax.ShapeDtypeStruct((16, r), F32))
        out_specs.append(pl.BlockSpec((16, tm), lambda i: (0, i)))
    else:
        out_shape.append(jax.ShapeDtypeStruct((r, 128), F32))
        out_specs.append(rows(128))
    in_specs = [rows(d), full((1, d)), full(w.shape), full(wf.shape), full(bf.shape), full((1, mix))]
    return pl.pallas_call(
        functools.partial(_in_proj_body, wa=wa, wb=wb, wc=wc, mix=mix, prompt=prompt),
        grid=(n,), in_specs=in_specs, out_specs=out_specs, out_shape=out_shape,
        compiler_params=_params(("parallel",)), name="in_proj_prompt" if prompt else "in_proj_sample",
    )(x2d, gpre, w, wf, bf, gout)


def _cum_body(lf_ref, k_ref, ka_ref, kn_ref, cl_ref, carry_ref, knmax_ref, *, tc, nh):
    @pl.when(pl.program_id(1) == 0)
    def _():
        carry_ref[...] = jnp.zeros_like(carry_ref)
        knmax_ref[...] = jnp.zeros_like(knmax_ref)

    upper = _mask01(_iota2((tc, tc), 0) <= _iota2((tc, tc), 1))
    c = _dot_exact_lhs(lf_ref[...], upper) + carry_ref[...]
    carry_ref[...] = c[:, tc - 1:tc]
    parts = [p.astype(F32) for p in _split_bf16(c, N_SPLIT)]
    parts = jnp.concatenate(parts + [jnp.zeros((8, tc), F32)], axis=0).astype(BF16)
    r = _iota2((parts.shape[0], LANES), 0)
    lane = _iota2((parts.shape[0], LANES), 1)
    in_extra = jnp.logical_and(lane >= DH, lane < DH + N_SPLIT)
    key_lane = _iota2((tc, LANES), 1)
    norms = []
    for hh in range(nh):
        sel = jnp.where(jnp.logical_and(in_extra, r == (lane - DH) * 8 + hh), -1.0, 0.0).astype(BF16)
        extra = lax.dot_general(parts, sel, TN, preferred_element_type=F32)
        kh = k_ref[hh]
        ka_ref[hh] = jnp.where(key_lane < DH, kh, extra.astype(BF16))
        kf = kh.astype(F32)
        norms.append(jnp.sqrt(jnp.max(jnp.sum(kf * kf, axis=1, keepdims=True), axis=0, keepdims=True)))
    knmax = jnp.maximum(knmax_ref[...], jnp.concatenate(norms + [jnp.zeros((8 - nh, 1), F32)], axis=0))
    knmax_ref[...] = knmax
    kn_ref[0] = jnp.broadcast_to(knmax, (8, tc))
    cl_ref[0] = jnp.broadcast_to(c[:, tc - 1:tc], (8, tc))


def _cum(lft, kpad, nseq, seqlen):
    tc = ROW_TILE
    nb = seqlen // tc
    nh = kpad.shape[0]
    return pl.pallas_call(
        functools.partial(_cum_body, tc=tc, nh=nh),
        grid=(nseq, nb),
        in_specs=[pl.BlockSpec((8, tc), lambda b, i: (0, b * nb + i)),
                  pl.BlockSpec((nh, tc, LANES), lambda b, i: (0, b * nb + i, 0))],
        out_specs=(pl.BlockSpec((nh, tc, LANES), lambda b, i: (0, b * nb + i, 0)),
                   pl.BlockSpec((1, 8, tc), lambda b, i: (b * nb + i, 0, 0)),
                   pl.BlockSpec((1, 8, tc), lambda b, i: (b * nb + i, 0, 0))),
        out_shape=(jax.ShapeDtypeStruct(kpad.shape, BF16),
                   jax.ShapeDtypeStruct((nseq * nb, 8, tc), F32),
                   jax.ShapeDtypeStruct((nseq * nb, 8, tc), F32)),
        scratch_shapes=[pltpu.VMEM((8, 1), F32), pltpu.VMEM((8, 1), F32)],
        compiler_params=_params(("parallel", "arbitrary")), name="fox_cum",
    )(lft, kpad)


def _hgrn_body(q_ref, f_ref, i_ref, lbp_ref, s0_ref, o_ref, sout_ref,
               st_ref, qin_ref, kout_ref, dec_ref, oin_ref, tri_ref, ds_ref, stb_ref,
               *, layer, nseq, nchunk, c, hg):
    step = pl.program_id(2)
    t = nchunk * c
    rows = nseq * t
    w = hg * DKA
    head = lambda hh: slice(hh * DKA, (hh + 1) * DKA)

    @pl.when(step == 0)
    def _():
        for s in range(nseq):
            for hh in range(hg):
                st_ref[s * hg + hh] = s0_ref[0, s, hh].T
        ri = _iota2((rows, rows), 0)
        ci = _iota2((rows, rows), 1)
        tri_ref[...] = _mask01(jnp.logical_and(_div_pow2(ri, c) == _div_pow2(ci, c), ci <= ri))

    lbp = lbp_ref[...]
    e = jnp.exp(lbp - jnp.max(lbp, axis=0, keepdims=True))
    sm = e / jnp.sum(e, axis=0, keepdims=True)
    lb = jnp.zeros((1, w), F32)
    for r in range(1, layer + 1):
        lb = lb + sm[r:r + 1, :]

    z = f_ref[...]
    qa = q_ref[...]
    q = qa * _sigmoid(qa)
    log_f = jnp.log(lb)
    other = jnp.log1p(-lb) + _log_sigmoid(z)
    mx = jnp.maximum(log_f, other)
    log_f = mx + jnp.log1p(jnp.exp(-jnp.abs(log_f - other)))
    k = (1.0 - lb) * _sigmoid(-z)

    tri = tri_ref[...]
    hi, lo = _split_bf16(log_f, 2)
    b2 = jnp.dot(tri, jnp.concatenate([hi, lo], axis=1), preferred_element_type=F32)
    b = b2[:, :w] + b2[:, w:]
    b3 = b.reshape(rows // c, c, w)
    b_rest = (jnp.broadcast_to(b3[:, c - 1:c, :], b3.shape) - b3).reshape(rows, w)
    q_in = q * jnp.exp(b)
    qin_ref[...] = q_in
    kout_ref[...] = k * jnp.exp(b_rest)
    dec_ref[...] = jnp.exp(b + b_rest)
    v_all = i_ref[...].astype(BF16)

    q_in_b = q_in.astype(BF16)
    k_in_b = (k * jnp.exp(-b)).astype(BF16)
    for hh in range(hg):
        a = lax.dot_general(q_in_b[:, head(hh)], k_in_b[:, head(hh)], NT, preferred_element_type=F32)
        a = jnp.where(tri > 0, a, 0.0)
        oin_ref[:, head(hh)] = jnp.dot(a.astype(BF16), v_all[:, head(hh)], preferred_element_type=F32)

    for idx in range(nseq * nchunk):
        lo = idx * c
        for hh in range(hg):
            ko = kout_ref[lo:lo + c, head(hh)].astype(BF16)
            vv = i_ref[lo:lo + c, head(hh)].astype(BF16)
            ds_ref[idx * hg + hh] = lax.dot_general(vv, ko, TN, preferred_element_type=F32)
    for s in range(nseq):
        for hh in range(hg):
            st = st_ref[s * hg + hh]
            for ch in range(nchunk):
                idx = s * nchunk + ch
                stb_ref[idx * hg + hh] = st.astype(BF16)
                st = st * dec_ref[idx * c:idx * c + 1, head(hh)] + ds_ref[idx * hg + hh]
            st_ref[s * hg + hh] = st
            sout_ref[s, hh] = st.T
    for idx in range(nseq * nchunk):
        lo = idx * c
        for hh in range(hg):
            qi = qin_ref[lo:lo + c, head(hh)].astype(BF16)
            o = oin_ref[lo:lo + c, head(hh)] + lax.dot_general(qi, stb_ref[idx * hg + hh], NT,
                                                               preferred_element_type=F32)
            o_ref[lo:lo + c, head(hh)] = o * lax.rsqrt(jnp.mean(o * o, axis=-1, keepdims=True) + EPS)


def _hgrn(ua, lb_param, s0, layer, s0_layer, *, nseq_total, seqlen, nseq, nchunk, c, ha):
    r = ua.shape[0]
    t = nchunk * c
    nsteps = seqlen // t
    rows = nseq * t
    depth = lb_param.shape[0]
    hg = HGRN_HEADS
    assert ha % hg == 0
    w = hg * DKA
    col = lambda off: pl.BlockSpec((rows, w), lambda sb, h, i: (sb * nsteps + i, off // hg + h))
    return pl.pallas_call(
        functools.partial(_hgrn_body, layer=layer, nseq=nseq, nchunk=nchunk, c=c, hg=hg),
        grid=(nseq_total // nseq, ha // hg, nsteps),
        in_specs=[col(0), col(ha), col(2 * ha),
                  pl.BlockSpec((depth, w), lambda sb, h, i: (0, h)),
                  pl.BlockSpec((1, nseq, hg, DKA, DVA), lambda sb, h, i: (s0_layer, sb, h, 0, 0))],
        out_specs=(pl.BlockSpec((rows, w), lambda sb, h, i: (sb * nsteps + i, h)),
                   pl.BlockSpec((nseq, hg, DKA, DVA), lambda sb, h, i: (sb, h, 0, 0))),
        out_shape=(jax.ShapeDtypeStruct((r, ha * DVA), F32),
                   jax.ShapeDtypeStruct((nseq_total, ha, DKA, DVA), F32)),
        scratch_shapes=[pltpu.VMEM((nseq * hg, DVA, DKA), F32)] + [pltpu.VMEM((rows, w), F32)] * 4
                       + [pltpu.VMEM((rows, rows), BF16), pltpu.VMEM((nseq * nchunk * hg, DVA, DKA), F32),
                          pltpu.VMEM((nseq * nchunk * hg, DVA, DKA), BF16)],
        compiler_params=_params(("parallel", "parallel", "arbitrary")), name="hgrn",
    )(ua, ua, ua, lb_param, s0)


def _fox_body(qt_ref, ka_ref, vt_ref, kn_ref, cl_ref, o_ref, m_ref, l_ref, acc_ref, *, tq):
    h = pl.program_id(1)
    i = pl.program_id(2)
    sub = tq // LANES
    qt = _lane_blocks([qt_ref[0, s] for s in range(sub)])
    ones = jnp.where(_iota2((LANES - DH, tq), 0) < N_SPLIT, 1.0, 0.0).astype(BF16)
    q_aug = jnp.concatenate([qt, ones], axis=0)
    qf = qt.astype(F32)
    qn = jnp.sqrt(jnp.sum(qf * qf, axis=0, keepdims=True))

    def live(j):
        bound = qn * kn_ref[j, pl.ds(h, 1), :] - cl_ref[j, pl.ds(h, 1), :] - m_ref[...]
        return jnp.max(bound)

    m_ref[...] = jnp.full_like(m_ref, NEG_BIG)
    l_ref[...] = jnp.zeros_like(l_ref)
    acc_ref[...] = jnp.zeros_like(acc_ref)

    def scores(j, nblk):
        ka = ka_ref[0, pl.ds(_aligned(j * tq, tq), nblk * tq), :]
        return jnp.dot(ka, q_aug, preferred_element_type=F32)

    def consume(s, j, nblk, masked):
        tk = nblk * tq
        vt = _lane_blocks([vt_ref[0, j * sub + t] for t in range(nblk * sub)])
        if masked:
            s = jnp.where(_iota2((tk, tq), 0) <= _iota2((tk, tq), 1), s, -jnp.inf)
        m_old = m_ref[...]
        m_new = jnp.maximum(m_old, jnp.max(s, axis=0, keepdims=True))
        alpha = jnp.exp(m_old - m_new)
        p = jnp.exp(s - m_new)
        l_ref[...] = alpha * l_ref[...] + jnp.sum(p, axis=0, keepdims=True)
        acc_ref[...] = alpha * acc_ref[...] + jnp.dot(vt, p.astype(BF16), preferred_element_type=F32)
        m_ref[...] = m_new

    consume(scores(i, 1), i, 1, True)

    nquad = lax.shift_right_logical(i, 2)

    def cond(c):
        kk, alive = c
        return jnp.logical_and(kk < nquad, alive > FOX_DEAD)

    def body(c):
        kk, _ = c
        ja = i - 2 - 4 * kk
        jb = ja - 2
        sa = scores(ja, 2)
        sb = scores(jb, 2)
        consume(sa, ja, 2, False)
        consume(sb, jb, 2, False)
        return kk + 1, live(jnp.maximum(jb - 1, 0))

    lax.while_loop(cond, body, (0, live(jnp.maximum(i - 1, 0))))
    rest = lax.bitwise_and(i, 3)
    rest_alive = live(jnp.maximum(rest - 1, 0)) > FOX_DEAD

    @pl.when(jnp.logical_and(rest >= 2, rest_alive))
    def _():
        j0 = lax.bitwise_and(rest, 1)
        consume(scores(j0, 2), j0, 2, False)

    @pl.when(jnp.logical_and(lax.bitwise_and(rest, 1) == 1, rest_alive))
    def _():
        consume(scores(0, 1), 0, 1, False)

    o_ref[0] = _rms_rows_to_cols(acc_ref[...] / l_ref[...])


def _fox(qt, ka, vt, kn, cl, *, nseq, seqlen):
    nh, r, _ = ka.shape
    tq = ROW_TILE
    nq = seqlen // tq
    sub = tq // LANES
    return pl.pallas_call(
        functools.partial(_fox_body, tq=tq),
        grid=(nseq, nh, nq),
        in_specs=[pl.BlockSpec((1, sub, DH, LANES), lambda b, h, i: (h, b * nq + i, 0, 0)),
                  pl.BlockSpec((1, seqlen, LANES), lambda b, h, i: (h, b, 0)),
                  pl.BlockSpec((1, seqlen // LANES, DH, LANES), lambda b, h, i: (h, b, 0, 0)),
                  pl.BlockSpec((nq, 8, tq), lambda b, h, i: (b, 0, 0)),
                  pl.BlockSpec((nq, 8, tq), lambda b, h, i: (b, 0, 0))],
        out_specs=pl.BlockSpec((1, tq, DH), lambda b, h, i: (h, b * nq + i, 0)),
        out_shape=jax.ShapeDtypeStruct((nh, r, DH), F32),
        scratch_shapes=[pltpu.VMEM((1, tq), F32), pltpu.VMEM((1, tq), F32), pltpu.VMEM((DH, tq), F32)],
        compiler_params=_params(("parallel", "parallel", "parallel")), name="fox",
    )(qt, ka, vt, kn, cl)


def _sb_body(qt_ref, k_ref, vt_ref, o_ref, carry_ref, acc_ref, *, tq):
    i = pl.program_id(2)
    ts = SB_SUB
    nsub = tq // ts
    per = ts // LANES
    qt = _lane_blocks([qt_ref[0, s] for s in range(tq // LANES)])
    later = _mask01(_iota2((ts, ts), 1) > _iota2((ts, ts), 0))
    later2 = jnp.concatenate([later, later], axis=1)

    def window(jj0, nw, lo, width, masked):
        rows = nw * ts
        lanes = slice(lo, lo + width)
        k = k_ref[0, pl.ds(_aligned(jj0 * ts, ts), rows), :]
        vt = _lane_blocks([vt_ref[0, jj0 * per + s] for s in range(nw * per)])
        z = jnp.dot(k, qt[:, lanes], preferred_element_type=F32)
        if masked:
            ahead = (_iota2((rows, width), 0) - _iota2((rows, width), 1)) < (i * tq + lo - jj0 * ts)
            z = jnp.where(ahead, z, NEG_BIG)
        sp = _softplus_scores(z)
        pieces = []
        newer = carry_ref[:, lanes]
        for s in reversed(range(nw)):
            sp_s = sp[s * ts:(s + 1) * ts, :]
            hi, lo_term = _split_bf16(sp_s, 2)
            behind = jnp.dot(later2, jnp.concatenate([hi, lo_term], axis=0), preferred_element_type=F32)
            pieces.append(behind + newer)
            newer = newer + behind[0:1, :] + sp_s[0:1, :]
        behind = pieces[0] if nw == 1 else jnp.concatenate(pieces[::-1], axis=0)
        w = jnp.exp(z - sp - behind)
        acc_ref[:, lanes] += jnp.dot(vt, w.astype(BF16), preferred_element_type=F32)
        carry_ref[:, lanes] = newer

    assert nsub == 2
    base = jnp.maximum(i * nsub - 1, 0)
    half = tq // 2
    tiles = [(0, 0), (1, 0), (0, half), (1, half), (2, half)]
    k_win = k_ref[0, pl.ds(_aligned(base * ts, ts), (nsub + 1) * ts), :]
    z_lo = jnp.dot(k_win[:nsub * ts], qt[:, :half], preferred_element_type=F32)
    z_hi = jnp.dot(k_win, qt[:, half:], preferred_element_type=F32)
    diff = _iota2((ts, half), 0) - _iota2((ts, half), 1)
    z_tiles = []
    for s, lo in tiles:
        zt = (z_lo if lo == 0 else z_hi)[s * ts:(s + 1) * ts, :]
        z_tiles.append(jnp.where(diff < (i * tq + lo - (base + s) * ts), zt, NEG_BIG))
    z = jnp.concatenate(z_tiles, axis=1)
    sp = _softplus_scores(z)
    hi, lo_term = _split_bf16(sp, 2)
    behind = jnp.dot(later2, jnp.concatenate([hi, lo_term], axis=0), preferred_element_type=F32)
    mass = behind[0:1, :] + sp[0:1, :]
    col = lambda t: slice(t * half, (t + 1) * half)
    zero = jnp.zeros((1, half), F32)
    newer = jnp.concatenate([mass[:, col(1)], zero,
                             mass[:, col(3)] + mass[:, col(4)], mass[:, col(4)], zero], axis=1)
    w = jnp.exp(z - sp - behind - newer).astype(BF16)
    out = [None, None]
    for t, (s, lo) in enumerate(tiles):
        vt = _lane_blocks([vt_ref[0, (base + s) * per + u] for u in range(per)])
        term = jnp.dot(vt, w[:, col(t)], preferred_element_type=F32)
        side = 0 if lo == 0 else 1
        out[side] = term if out[side] is None else out[side] + term
    acc_ref[...] = jnp.concatenate(out, axis=1)
    carry_ref[...] = jnp.concatenate([mass[:, col(0)] + mass[:, col(1)],
                                      mass[:, col(2)] + mass[:, col(3)] + mass[:, col(4)]], axis=1)

    for lo in (0, half):
        def cond(c):
            jj, live = c
            return jnp.logical_and(jj >= 0, live < -SB_DEAD)

        def body(c, lo=lo):
            jj, _ = c
            window(jj, 1, lo, half, False)
            return jj - 1, jnp.min(carry_ref[:, lo:lo + half])

        lax.while_loop(cond, body, (i * nsub - 2, jnp.min(carry_ref[:, lo:lo + half])))
    o_ref[0] = _rms_rows_to_cols(acc_ref[...])


def _sb(qt, k, vt, *, nseq, seqlen):
    nh, r, _ = k.shape
    tq = ROW_TILE
    nq = seqlen // tq
    assert seqlen >= tq + SB_SUB
    return pl.pallas_call(
        functools.partial(_sb_body, tq=tq),
        grid=(nseq, nh, nq),
        in_specs=[pl.BlockSpec((1, tq // LANES, DH, LANES), lambda b, h, i: (h, b * nq + i, 0, 0)),
                  pl.BlockSpec((1, seqlen, DH), lambda b, h, i: (h, b, 0)),
                  pl.BlockSpec((1, seqlen // LANES, DH, LANES), lambda b, h, i: (h, b, 0, 0))],
        out_specs=pl.BlockSpec((1, tq, DH), lambda b, h, i: (h, b * nq + i, 0)),
        out_shape=jax.ShapeDtypeStruct((nh, r, DH), F32),
        scratch_shapes=[pltpu.VMEM((1, tq), F32), pltpu.VMEM((DH, tq), F32)],
        compiler_params=_params(("parallel", "parallel", "parallel")), name="stickbreak",
    )(qt, k, vt)


def _dec_body(pt_ref, fk_hbm, fv_hbm, lf_hbm, ck_hbm, cv_hbm,
              qb_ref, kb_ref, vb_ref, lfn_ref, qc_ref, kc_ref, vc_ref,
              ob_ref, oc_ref,
              fk_buf, fv_buf, lf_buf, ck_buf, cv_buf, xk_buf, xv_buf, o_acc, carry_ref, sem_pre, sem_x,
              *, layer, npages, page, nq, hb, hc, group):
    n = npages
    npre = min(SB_PREFETCH, n)
    step = pl.program_id(0)
    slot = lax.rem(step, 2)

    def prefetch(step, slot):
        out = []
        for g in range(group):
            seq = step * group + g
            for p in range(n):
                pg = pt_ref[seq, p]
                dst = pl.ds(p * page, page)
                out.append(pltpu.make_async_copy(fk_hbm.at[layer, pg], fk_buf.at[slot, g, :, dst], sem_pre.at[slot]))
                out.append(pltpu.make_async_copy(fv_hbm.at[layer, pg], fv_buf.at[slot, g, :, dst], sem_pre.at[slot]))
                out.append(pltpu.make_async_copy(lf_hbm.at[layer, pg], lf_buf.at[slot, g, p], sem_pre.at[slot]))
            for t in range(npre):
                pg = pt_ref[seq, n - npre + t]
                dst = pl.ds(t * page, page)
                out.append(pltpu.make_async_copy(ck_hbm.at[layer, pg], ck_buf.at[slot, g, :, dst], sem_pre.at[slot]))
                out.append(pltpu.make_async_copy(cv_hbm.at[layer, pg], cv_buf.at[slot, g, :, dst], sem_pre.at[slot]))
        return out

    @pl.when(step == 0)
    def _():
        for cp in prefetch(0, 0):
            cp.start()

    @pl.when(step + 1 < pl.num_programs(0))
    def _():
        for cp in prefetch(step + 1, 1 - slot):
            cp.start()

    for cp in prefetch(step, slot):
        cp.wait()

    def q_rows(q, nh):
        w = nh * DH
        qt = jnp.concatenate([q] * nh, axis=0)
        keep = _div_pow2(_iota2((nh * nq, w), 1), DH) == _div_pow2(_iota2((nh * nq, w), 0), nq)
        return jnp.where(keep, qt, 0.0).astype(BF16)

    def new_rows(ref, g):
        return ref[g * nq:(g + 1) * nq, :]

    def new_page(ref, g):
        x = new_rows(ref, g)
        return jnp.concatenate([x, jnp.zeros((page - nq, x.shape[1]), F32)], axis=0).astype(BF16)

    def per_query_rows(c):
        return jnp.concatenate([jnp.broadcast_to(c[i:i + 1, :], (nq, c.shape[1])) for i in range(c.shape[0])], axis=0)

    def stack(per_seq):
        return per_seq[0] if group == 1 else jnp.concatenate(per_seq, axis=0)

    def write_heads(o_full, nh, o_ref, g):
        for hh in range(nh):
            blk = o_full[hh * nq:(hh + 1) * nq, hh * DH:(hh + 1) * DH]
            o_ref[hh, g * nq:(g + 1) * nq, :] = blk * lax.rsqrt(jnp.mean(blk * blk, axis=-1, keepdims=True) + EPS)

    assert hb == hc
    rows = hb * nq
    upper = _mask01(_iota2((page, page), 0) <= _iota2((page, page), 1))
    later = _mask01(_iota2((page, page), 0) > _iota2((page, page), 1))
    key = _iota2((group * rows, page), 1)
    q_of_row = _mod_pow2(_iota2((group * rows, page), 0), nq)

    qr = [q_rows(new_rows(qb_ref, g), hb) for g in range(group)]
    lf_all = jnp.concatenate([lf_buf[slot, g, p] for p in range(n) for g in range(group)], axis=0)
    within = _dot_exact_lhs(lf_all, upper)
    carry = jnp.zeros((group * hb, 1), F32)
    c_pages = []
    for p in range(n):
        c = within[p * group * hb:(p + 1) * group * hb, :] + carry
        carry = c[:, page - 1:page]
        c_pages.append(per_query_rows(c))
    pick = _mask01(_iota2((rows, 128), 1) == _div_pow2(_iota2((rows, 128), 0), nq))
    c_new = []
    for g in range(group):
        lf_new = new_rows(lfn_ref, g)
        run = [lf_new[0:1, :]]
        for t in range(1, nq):
            run.append(run[-1] + lf_new[t:t + 1, :])
        c_rows = jnp.concatenate(run + [jnp.zeros((page - nq, lf_new.shape[1]), F32)], axis=0)
        acc = None
        for part in _split_bf16(c_rows, 3):
            term = lax.dot_general(pick, part, NT, preferred_element_type=F32)
            acc = term if acc is None else acc + term
        c_new.append(acc)
    c_new = stack(c_new) + per_query_rows(carry)
    s_past = stack([jnp.dot(qr[g], fk_buf[slot, g].astype(BF16), preferred_element_type=F32) for g in range(group)])
    s_past = s_past - jnp.concatenate(c_pages, axis=1)
    s_new = stack([lax.dot_general(qr[g], new_page(kb_ref, g), NT, preferred_element_type=F32)
                   for g in range(group)]) - c_new
    s_new = jnp.where(key <= q_of_row, s_new, -jnp.inf)
    m = jnp.maximum(jnp.max(s_past, axis=-1, keepdims=True), jnp.max(s_new, axis=-1, keepdims=True))
    p_past = jnp.exp(s_past - m)
    p_new = jnp.exp(s_new - m)
    inv_l = 1.0 / (jnp.sum(p_past, axis=-1, keepdims=True) + jnp.sum(p_new, axis=-1, keepdims=True))
    p_past = p_past.astype(BF16)
    p_new = p_new.astype(BF16)
    for g in range(group):
        sl = slice(g * rows, (g + 1) * rows)
        o = lax.dot_general(p_past[sl], fv_buf[slot, g].astype(BF16), NT, preferred_element_type=F32)
        o = o + jnp.dot(p_new[sl], new_page(vb_ref, g), preferred_element_type=F32)
        write_heads(o * inv_l[sl], hb, ob_ref, g)

    def sb_pages(z_pages, v_of_page, mask_newest, carry_in):
        out = None
        newer = carry_in
        r = z_pages[0].shape[0]
        for idx in reversed(range(len(z_pages))):
            z = z_pages[idx]
            sp = _softplus_scores(z)
            masked = mask_newest and idx == len(z_pages) - 1
            log_keep = jnp.where(key[:r] < q_of_row[:r], -sp, 0.0) if masked else -sp
            after = _dot_exact_lhs(log_keep, later, n=2) + newer
            w = jnp.exp(z - sp + after)
            if masked:
                w = jnp.where(key[:r] < q_of_row[:r], w, 0.0)
            term = v_of_page(idx, w.astype(BF16))
            out = term if out is None else out + term
            newer = newer + jnp.sum(log_keep, axis=-1, keepdims=True)
        return out, newer

    qr = [q_rows(new_rows(qc_ref, g), hc) for g in range(group)]
    cv = [cv_buf[slot, g].astype(BF16) for g in range(group)]
    v_new = [new_page(vc_ref, g) for g in range(group)]
    z_pre = stack([jnp.dot(qr[g], ck_buf[slot, g].astype(BF16), preferred_element_type=F32) for g in range(group)])
    z_pages = [z_pre[:, t * page:(t + 1) * page] for t in range(npre)]
    z_pages.append(stack([lax.dot_general(qr[g], new_page(kc_ref, g), NT, preferred_element_type=F32)
                          for g in range(group)]))

    def v_pre(idx, w):
        outs = []
        for g in range(group):
            wg = w[g * rows:(g + 1) * rows]
            if idx == npre:
                outs.append(jnp.dot(wg, v_new[g], preferred_element_type=F32))
            else:
                outs.append(lax.dot_general(wg, cv[g][:, idx * page:(idx + 1) * page], NT,
                                            preferred_element_type=F32))
        return stack(outs)

    o, carry = sb_pages(z_pages, v_pre, True, jnp.zeros((group * rows, 1), F32))
    o_acc[...] = o
    carry_ref[...] = carry

    for g in range(group):
        sl = slice(g * rows, (g + 1) * rows)
        seq = step * group + g

        def cond(c):
            p, live = c
            return jnp.logical_and(p >= 0, live > SB_DEAD)

        def body(c, g=g, sl=sl, seq=seq):
            p, _ = c
            pg = pt_ref[seq, p]
            cp_k = pltpu.make_async_copy(ck_hbm.at[layer, pg], xk_buf, sem_x.at[0])
            cp_v = pltpu.make_async_copy(cv_hbm.at[layer, pg], xv_buf, sem_x.at[1])
            cp_k.start()
            cp_v.start()
            cp_k.wait()
            cp_v.wait()
            z = jnp.dot(qr[g], xk_buf[...].astype(BF16), preferred_element_type=F32)
            xv = xv_buf[...].astype(BF16)
            o_p, newer = sb_pages([z], lambda idx, w: lax.dot_general(w, xv, NT, preferred_element_type=F32),
                                  False, carry_ref[sl, :])
            o_acc[sl, :] += o_p
            carry_ref[sl, :] = newer
            return p - 1, jnp.max(newer)

        lax.while_loop(cond, body, (n - npre - 1, jnp.max(carry_ref[sl, :])))
        write_heads(o_acc[sl, :], hc, oc_ref, g)


def _dec(page_table, layer, cache_fox_k, cache_fox_v, cache_fox_logf, cache_sb_k, cache_sb_v,
         qb, kb, vb, lfr, qc, kc, vc, *, nq):
    nseq, npages = page_table.shape
    page = cache_fox_k.shape[3]
    wb, wc = cache_fox_k.shape[2], cache_sb_k.shape[2]
    hb, hc = wb // DH, wc // DH
    npre = min(SB_PREFETCH, npages)
    group = DEC_GROUP
    assert nseq % group == 0
    hbm = pl.BlockSpec(memory_space=pl.ANY)
    new = lambda width: pl.BlockSpec((group * nq, width), lambda b, pt: (b, 0))
    heads = lambda nh: pl.BlockSpec((nh, group * nq, DH), lambda b, pt: (0, b, 0))
    grid_spec = pltpu.PrefetchScalarGridSpec(
        num_scalar_prefetch=1, grid=(nseq // group,),
        in_specs=[hbm] * 5 + [new(wb), new(wb), new(wb), new(128), new(wc), new(wc), new(wc)],
        out_specs=(heads(hb), heads(hc)),
        scratch_shapes=[pltpu.VMEM((2, group, wb, npages * page), F32),
                        pltpu.VMEM((2, group, wb, npages * page), F32),
                        pltpu.VMEM((2, group, npages, hb, page), F32),
                        pltpu.VMEM((2, group, wc, npre * page), F32), pltpu.VMEM((2, group, wc, npre * page), F32),
                        pltpu.VMEM((wc, page), F32), pltpu.VMEM((wc, page), F32),
                        pltpu.VMEM((group * hc * nq, wc), F32), pltpu.VMEM((group * hc * nq, 1), F32),
                        pltpu.SemaphoreType.DMA((2,)), pltpu.SemaphoreType.DMA((2,))])
    return pl.pallas_call(
        functools.partial(_dec_body, layer=layer, npages=npages, page=page, nq=nq, hb=hb, hc=hc, group=group),
        grid_spec=grid_spec,
        out_shape=(jax.ShapeDtypeStruct((hb, nseq * nq, DH), F32),
                   jax.ShapeDtypeStruct((hc, nseq * nq, DH), F32)),
        compiler_params=_params(("arbitrary",)), name="paged_decode",
    )(page_table, cache_fox_k, cache_fox_v, cache_fox_logf, cache_sb_k, cache_sb_v, qb, kb, vb, lfr, qc, kc, vc)


def _out_proj_body(oa_ref, ob_ref, oc_ref, gate_ref, w_ref, x_ref, gpost_ref, y_ref):
    parts = [oa_ref[...]]
    parts += [ob_ref[hh] for hh in range(ob_ref.shape[0])]
    parts += [oc_ref[hh] for hh in range(oc_ref.shape[0])]
    o = jnp.concatenate(parts, axis=-1) * gate_ref[...]
    y = jnp.dot(o.astype(BF16), w_ref[...], preferred_element_type=F32)
    y = y * lax.rsqrt(jnp.mean(y * y, axis=-1, keepdims=True) + EPS) * gpost_ref[...]
    y_ref[...] = x_ref[...] + y


def _out_proj(oa, ob, oc, gate, w, x2d, gpost):
    r, d = x2d.shape
    tm = ROW_TILE
    rows = lambda width: pl.BlockSpec((tm, width), lambda i: (i, 0))
    heads = lambda nh: pl.BlockSpec((nh, tm, DH), lambda i: (0, i, 0))
    full = lambda shape: pl.BlockSpec(shape, lambda i: (0,) * len(shape))
    return pl.pallas_call(
        _out_proj_body, grid=(r // tm,),
        in_specs=[rows(oa.shape[1]), heads(ob.shape[0]), heads(oc.shape[0]), rows(gate.shape[1]),
                  full(w.shape), rows(d), full((1, d))],
        out_specs=rows(d), out_shape=jax.ShapeDtypeStruct((r, d), F32),
        compiler_params=_params(("parallel",)), name="out_proj",
    )(oa, ob, oc, gate, w, x2d, gpost)


def kernel(x_prompt, x_sample, cache_fox_k, cache_fox_v, cache_fox_logf, cache_sb_k, cache_sb_v,
           state_hgrn, page_table, w_in, b_fox, lb_param, w_out, g_pre, g_post, g_out):
    depth, d, n_in = w_in.shape
    mix = w_out.shape[1]
    hb = b_fox.shape[1]
    wa = lb_param.shape[1]
    wb = hb * DH
    wc = mix - wa - wb
    hc = wc // DH
    ha = wa // DKA
    assert n_in == 4 * wa + 4 * wb + hb + 4 * wc
    bp, lp, _ = x_prompt.shape
    bs, ls, _ = x_sample.shape
    n_phys, page = cache_fox_k.shape[1], cache_fox_k.shape[2]
    assert lp % ROW_TILE == 0 and (bs * ls) % ROW_TILE == 0 and HGRN_CHUNK % ls == 0

    o_qb = 4 * wa
    o_fb = o_qb + 4 * wb
    o_qc = o_fb + hb
    w_perm = jnp.concatenate([
        w_in[:, :, 0:3 * wa], w_in[:, :, 3 * wa:4 * wa],
        w_in[:, :, o_qb + 3 * wb:o_qb + 4 * wb], w_in[:, :, o_qc + 3 * wc:o_qc + 4 * wc],
        w_in[:, :, o_qb:o_qb + 3 * wb], w_in[:, :, o_qc:o_qc + 3 * wc]], axis=-1).astype(BF16)
    w_f = w_in[:, :, o_fb:o_fb + hb]
    w_fr = jnp.pad(w_f, ((0, 0), (0, 0), (0, 128 - hb))).astype(BF16)
    w_ft = jnp.pad(jnp.swapaxes(w_f, 1, 2), ((0, 0), (0, 16 - hb), (0, 0))).astype(BF16)
    b_fr = jnp.pad(b_fox, ((0, 0), (0, 128 - hb)))[:, None, :]
    b_ft = jnp.pad(b_fox, ((0, 0), (0, 16 - hb)))[:, :, None]
    w_out_b = w_out.astype(BF16)

    pages_t = [jnp.transpose(a, (0, 1, 3, 4, 2)).reshape(depth, n_phys, -1, page)
               for a in (cache_fox_k, cache_fox_v, cache_sb_k, cache_sb_v)]
    logf_t = jnp.transpose(cache_fox_logf, (0, 1, 3, 2))
    zero_state = jnp.zeros((1, bp, ha, DKA, DVA), F32)

    xp = x_prompt.reshape(bp * lp, d)
    xs = x_sample.reshape(bs * ls, d)
    p_out = [[] for _ in range(6)]
    s_out = [[] for _ in range(6)]
    for l in range(depth):
        proj = functools.partial(_in_proj, gpre=g_pre[l][None, :], w=w_perm[l], gout=g_out[l][None, :],
                                 wa=wa, wb=wb, wc=wc)

        (ua, gate, qbt, kbp, vbt, kbt32, vbt32, qct, kcr, vct, kct32, vct32, lft) = proj(
            xp, wf=w_ft[l], bf=b_ft[l], prompt=True, nseq=bp)
        oa, s_fin = _hgrn(ua, lb_param, zero_state, l, 0, nseq_total=bp, seqlen=lp, nseq=1,
                          nchunk=ROW_TILE // HGRN_CHUNK, c=HGRN_CHUNK, ha=ha)
        kba, kn, cl = _cum(lft, kbp, bp, lp)
        ob = _fox(qbt, kba, vbt, kn, cl, nseq=bp, seqlen=lp)
        oc = _sb(qct, kcr, vct, nseq=bp, seqlen=lp)
        xp = _out_proj(oa, ob, oc, gate, w_out_b[l], xp, g_post[l][None, :])
        for lst, a in zip(p_out, (kbt32, vbt32, lft[:hb].reshape(hb, bp, lp), kct32, vct32, s_fin)):
            lst.append(a)

        ua, gate, qb, kb, vb, qc, kc, vc, lfr = proj(xs, wf=w_fr[l], bf=b_fr[l], prompt=False)
        oa, s_fin = _hgrn(ua, lb_param, state_hgrn, l, l, nseq_total=bs, seqlen=ls, nseq=16,
                          nchunk=1, c=ls, ha=ha)
        ob, oc = _dec(page_table, l, pages_t[0], pages_t[1], logf_t, pages_t[2], pages_t[3],
                      qb, kb, vb, lfr, qc, kc, vc, nq=ls)
        xs = _out_proj(oa, ob, oc, gate, w_out_b[l], xs, g_post[l][None, :])
        for lst, a in zip(s_out, (kb.reshape(bs, ls, hb, DH), vb.reshape(bs, ls, hb, DH),
                                  lfr[:, :hb].reshape(bs, ls, hb), kc.reshape(bs, ls, hc, DH),
                                  vc.reshape(bs, ls, hc, DH), s_fin)):
            lst.append(a)

    p_st = [jnp.stack(a, axis=0) for a in p_out]
    s_st = [jnp.stack(a, axis=0) for a in s_out]
    for idx in (0, 1, 3, 4):
        p_st[idx] = jnp.transpose(p_st[idx], (0, 1, 4, 2, 3))
    p_st[2] = jnp.transpose(p_st[2], (0, 2, 3, 1))
    return (xp.reshape(bp, lp, d), xs.reshape(bs, ls, d), *p_st, *s_st)
```

```python
import functools

import jax
import jax.numpy as jnp
from jax import lax
from jax.experimental import pallas as pl
from jax.experimental.pallas import tpu as pltpu

F32 = jnp.float32
BF16 = jnp.bfloat16

EPS = 1e-6
DKA = 128
DVA = 128
DH = 64
HGRN_CHUNK = 32
HGRN_HEADS = 4
NEG_BIG = -1e30
SB_DEAD = -104.0
FOX_DEAD = -106.0

LANES = 128
ROW_TILE = 512
SB_SUB = 256
SB_PREFETCH = 2
DEC_GROUP = 4
N_SPLIT = 3
VMEM_LIMIT = 56 * 1024 * 1024

NT = (((1,), (1,)), ((), ()))
TN = (((0,), (0,)), ((), ()))


def _params(sem):
    return pltpu.CompilerParams(dimension_semantics=sem, vmem_limit_bytes=VMEM_LIMIT)


def _softplus(x):
    return jnp.maximum(x, 0.0) + jnp.log1p(jnp.exp(-jnp.abs(x)))


def _softplus_scores(x):
    return jnp.maximum(x, 0.0) + jnp.log(1.0 + jnp.exp(-jnp.abs(x)))


def _log_sigmoid(x):
    return -_softplus(-x)


def _sigmoid(x):
    return 1.0 / (1.0 + jnp.exp(-x))


def _split_bf16(x, n):
    parts = []
    r = x
    for t in range(n):
        p = r.astype(BF16)
        parts.append(p)
        if t + 1 < n:
            r = r - p.astype(F32)
    return parts


def _dot_exact_lhs(x, m01, n=3):
    acc = None
    for p in _split_bf16(x, n):
        t = jnp.dot(p, m01, preferred_element_type=F32)
        acc = t if acc is None else acc + t
    return acc


def _dot_exact_rhs(m01, x, n=3):
    acc = None
    for p in _split_bf16(x, n):
        t = jnp.dot(m01, p, preferred_element_type=F32)
        acc = t if acc is None else acc + t
    return acc


def _iota2(shape, dim):
    return lax.broadcasted_iota(jnp.int32, shape, dim)


def _mask01(cond):
    return jnp.where(cond, 1.0, 0.0).astype(BF16)


def _div_pow2(x, d):
    assert d & (d - 1) == 0
    return lax.shift_right_logical(x, d.bit_length() - 1)


def _mod_pow2(x, d):
    assert d & (d - 1) == 0
    return lax.bitwise_and(x, d - 1)


def _aligned(start, m):
    return start if isinstance(start, int) else pl.multiple_of(start, m)


def _lane_blocks(ref_blocks):
    return ref_blocks[0] if len(ref_blocks) == 1 else jnp.concatenate(ref_blocks, axis=1)


def _rms_rows_to_cols(o_t):
    o_t = o_t * lax.rsqrt(jnp.mean(o_t * o_t, axis=0, keepdims=True) + EPS)
    padded = jnp.concatenate([o_t, jnp.zeros((LANES - DH, o_t.shape[1]), F32)], axis=0)
    return padded.T[:, :DH]


def _in_proj_body(x_ref, gpre_ref, w_ref, wf_ref, bf_ref, gout_ref, ua_ref, gate_ref, *out_refs,
                  wa, wb, wc, mix, prompt):
    x = x_ref[...]
    tm = x.shape[0]
    h = x * lax.rsqrt(jnp.mean(x * x, axis=-1, keepdims=True) + EPS) * gpre_ref[...]
    hb = h.astype(BF16)

    def proj(lo, hi):
        return jnp.dot(hb, w_ref[:, lo:hi], preferred_element_type=F32)

    ua_ref[...] = proj(0, 3 * wa)
    g = proj(3 * wa, 3 * wa + mix)
    gate_ref[...] = g * _sigmoid(g) * gout_ref[...]

    def attn_group(off, w, refs, pad_k):
        u = proj(off, off + 3 * w)
        q = u[:, :w] * (DH ** -0.5)
        k = u[:, w:2 * w]
        v = u[:, 2 * w:3 * w]
        if not prompt:
            q32_ref, k32_ref, v32_ref = refs
            q32_ref[...] = q
            k32_ref[...] = k
            v32_ref[...] = v
            return
        qt_ref, k_ref, vt_ref, kt32_ref, vt32_ref = refs
        qt = q.T
        kt = k.T
        vt = v.T
        for hh in range(w // DH):
            rows = slice(hh * DH, (hh + 1) * DH)
            kh = k[:, rows]
            if pad_k:
                kh = jnp.concatenate([kh, jnp.zeros((tm, LANES - DH), F32)], axis=1)
            k_ref[hh] = kh.astype(BF16)
            kt32_ref[0, hh] = kt[rows, :]
            vt32_ref[0, hh] = vt[rows, :]
            for jj in range(tm // LANES):
                cols = slice(jj * LANES, (jj + 1) * LANES)
                qt_ref[hh, jj] = qt[rows, cols].astype(BF16)
                vt_ref[hh, jj] = vt[rows, cols].astype(BF16)

    per_group = 5 if prompt else 3
    off_b = 3 * wa + mix
    attn_group(off_b, wb, out_refs[0:per_group], True)
    attn_group(off_b + 3 * wb, wc, out_refs[per_group:2 * per_group], False)

    lf_ref = out_refs[2 * per_group]
    if prompt:
        f = lax.dot_general(wf_ref[...], hb, NT, preferred_element_type=F32)
    else:
        f = jnp.dot(hb, wf_ref[...], preferred_element_type=F32)
    lf_ref[...] = _log_sigmoid(f + bf_ref[...])


def _in_proj(x2d, gpre, w, wf, bf, gout, *, wa, wb, wc, prompt, nseq=1):
    r, d = x2d.shape
    tm = ROW_TILE
    mix = wa + wb + wc
    n = r // tm
    per_seq = n // nseq
    seqlen = r // nseq
    sub = tm // LANES
    full = lambda shape: pl.BlockSpec(shape, lambda i: (0,) * len(shape))
    rows = lambda width: pl.BlockSpec((tm, width), lambda i: (i, 0))
    out_shape = [jax.ShapeDtypeStruct((r, 3 * wa), F32),
                 jax.ShapeDtypeStruct((r, mix), F32)]
    out_specs = [rows(3 * wa), rows(mix)]
    for w_g, kw in ((wb, LANES), (wc, DH)):
        nh = w_g // DH
        if prompt:
            blocked = pl.BlockSpec((nh, sub, DH, LANES), lambda i: (0, i, 0, 0))
            tposed = pl.BlockSpec((1, nh, DH, tm), lambda i: (i // per_seq, 0, 0, i % per_seq))
            out_shape += [jax.ShapeDtypeStruct((nh, r // LANES, DH, LANES), BF16),
                          jax.ShapeDtypeStruct((nh, r, kw), BF16),
                          jax.ShapeDtypeStruct((nh, r // LANES, DH, LANES), BF16),
                          jax.ShapeDtypeStruct((nseq, nh, DH, seqlen), F32),
                          jax.ShapeDtypeStruct((nseq, nh, DH, seqlen), F32)]
            out_specs += [blocked, pl.BlockSpec((nh, tm, kw), lambda i: (0, i, 0)), blocked, tposed, tposed]
        else:
            out_shape += [jax.ShapeDtypeStruct((r, w_g), F32)] * 3
            out_specs += [rows(w_g)] * 3
    if prompt:
        out_shape.append(jax.ShapeDtypeStruct((16, r), F32))
        out_specs.append(pl.BlockSpec((16, tm), lambda i: (0, i)))
    else:
        out_shape.append(jax.ShapeDtypeStruct((r, 128), F32))
        out_specs.append(rows(128))
    in_specs = [rows(d), full((1, d)), full(w.shape), full(wf.shape), full(bf.shape), full((1, mix))]
    return pl.pallas_call(
        functools.partial(_in_proj_body, wa=wa, wb=wb, wc=wc, mix=mix, prompt=prompt),
        grid=(n,), in_specs=in_specs, out_specs=out_specs, out_shape=out_shape,
        compiler_params=_params(("parallel",)), name="in_proj_prompt" if prompt else "in_proj_sample",
    )(x2d, gpre, w, wf, bf, gout)


def _cum_body(lf_ref, k_ref, ka_ref, kn_ref, cl_ref, carry_ref, knmax_ref, *, tc, nh):
    @pl.when(pl.program_id(1) == 0)
    def _():
        carry_ref[...] = jnp.zeros_like(carry_ref)
        knmax_ref[...] = jnp.zeros_like(knmax_ref)

    upper = _mask01(_iota2((tc, tc), 0) <= _iota2((tc, tc), 1))
    c = _dot_exact_lhs(lf_ref[...], upper) + carry_ref[...]
    carry_ref[...] = c[:, tc - 1:tc]
    parts = [p.astype(F32) for p in _split_bf16(c, N_SPLIT)]
    parts = jnp.concatenate(parts + [jnp.zeros((8, tc), F32)], axis=0).astype(BF16)
    r = _iota2((parts.shape[0], LANES), 0)
    lane = _iota2((parts.shape[0], LANES), 1)
    in_extra = jnp.logical_and(lane >= DH, lane < DH + N_SPLIT)
    key_lane = _iota2((tc, LANES), 1)
    norms = []
    for hh in range(nh):
        sel = jnp.where(jnp.logical_and(in_extra, r == (lane - DH) * 8 + hh), -1.0, 0.0).astype(BF16)
        extra = lax.dot_general(parts, sel, TN, preferred_element_type=F32)
        kh = k_ref[hh]
        ka_ref[hh] = jnp.where(key_lane < DH, kh, extra.astype(BF16))
        kf = kh.astype(F32)
        norms.append(jnp.sqrt(jnp.max(jnp.sum(kf * kf, axis=1, keepdims=True), axis=0, keepdims=True)))
    knmax = jnp.maximum(knmax_ref[...], jnp.concatenate(norms + [jnp.zeros((8 - nh, 1), F32)], axis=0))
    knmax_ref[...] = knmax
    kn_ref[0] = jnp.broadcast_to(knmax, (8, tc))
    cl_ref[0] = jnp.broadcast_to(c[:, tc - 1:tc], (8, tc))


def _cum(lft, kpad, nseq, seqlen):
    tc = ROW_TILE
    nb = seqlen // tc
    nh = kpad.shape[0]
    return pl.pallas_call(
        functools.partial(_cum_body, tc=tc, nh=nh),
        grid=(nseq, nb),
        in_specs=[pl.BlockSpec((8, tc), lambda b, i: (0, b * nb + i)),
                  pl.BlockSpec((nh, tc, LANES), lambda b, i: (0, b * nb + i, 0))],
        out_specs=(pl.BlockSpec((nh, tc, LANES), lambda b, i: (0, b * nb + i, 0)),
                   pl.BlockSpec((1, 8, tc), lambda b, i: (b * nb + i, 0, 0)),
                   pl.BlockSpec((1, 8, tc), lambda b, i: (b * nb + i, 0, 0))),
        out_shape=(jax.ShapeDtypeStruct(kpad.shape, BF16),
                   jax.ShapeDtypeStruct((nseq * nb, 8, tc), F32),
                   jax.ShapeDtypeStruct((nseq * nb, 8, tc), F32)),
        scratch_shapes=[pltpu.VMEM((8, 1), F32), pltpu.VMEM((8, 1), F32)],
        compiler_params=_params(("parallel", "arbitrary")), name="fox_cum",
    )(lft, kpad)


def _hgrn_body(q_ref, f_ref, i_ref, lbp_ref, s0_ref, o_ref, sout_ref,
               st_ref, qin_ref, kout_ref, dec_ref, oin_ref, tri_ref, ds_ref, stb_ref,
               *, layer, nseq, nchunk, c, hg):
    step = pl.program_id(2)
    t = nchunk * c
    rows = nseq * t
    w = hg * DKA
    head = lambda hh: slice(hh * DKA, (hh + 1) * DKA)

    @pl.when(step == 0)
    def _():
        for s in range(nseq):
            for hh in range(hg):
                st_ref[s * hg + hh] = s0_ref[0, s, hh].T
        ri = _iota2((rows, rows), 0)
        ci = _iota2((rows, rows), 1)
        tri_ref[...] = _mask01(jnp.logical_and(_div_pow2(ri, c) == _div_pow2(ci, c), ci <= ri))

    lbp = lbp_ref[...]
    e = jnp.exp(lbp - jnp.max(lbp, axis=0, keepdims=True))
    sm = e / jnp.sum(e, axis=0, keepdims=True)
    lb = jnp.zeros((1, w), F32)
    for r in range(1, layer + 1):
        lb = lb + sm[r:r + 1, :]

    z = f_ref[...]
    qa = q_ref[...]
    q = qa * _sigmoid(qa)
    log_f = jnp.log(lb)
    other = jnp.log1p(-lb) + _log_sigmoid(z)
    mx = jnp.maximum(log_f, other)
    log_f = mx + jnp.log1p(jnp.exp(-jnp.abs(log_f - other)))
    k = (1.0 - lb) * _sigmoid(-z)

    tri = tri_ref[...]
    hi, lo = _split_bf16(log_f, 2)
    b2 = jnp.dot(tri, jnp.concatenate([hi, lo], axis=1), preferred_element_type=F32)
    b = b2[:, :w] + b2[:, w:]
    b3 = b.reshape(rows // c, c, w)
    b_rest = (jnp.broadcast_to(b3[:, c - 1:c, :], b3.shape) - b3).reshape(rows, w)
    q_in = q * jnp.exp(b)
    qin_ref[...] = q_in
    kout_ref[...] = k * jnp.exp(b_rest)
    dec_ref[...] = jnp.exp(b + b_rest)
    v_all = i_ref[...].astype(BF16)

    q_in_b = q_in.astype(BF16)
    k_in_b = (k * jnp.exp(-b)).astype(BF16)
    for hh in range(hg):
        a = lax.dot_general(q_in_b[:, head(hh)], k_in_b[:, head(hh)], NT, preferred_element_type=F32)
        a = jnp.where(tri > 0, a, 0.0)
        oin_ref[:, head(hh)] = jnp.dot(a.astype(BF16), v_all[:, head(hh)], preferred_element_type=F32)

    for idx in range(nseq * nchunk):
        lo = idx * c
        for hh in range(hg):
            ko = kout_ref[lo:lo + c, head(hh)].astype(BF16)
            vv = i_ref[lo:lo + c, head(hh)].astype(BF16)
            ds_ref[idx * hg + hh] = lax.dot_general(vv, ko, TN, preferred_element_type=F32)
    for s in range(nseq):
        for hh in range(hg):
            st = st_ref[s * hg + hh]
            for ch in range(nchunk):
                idx = s * nchunk + ch
                stb_ref[idx * hg + hh] = st.astype(BF16)
                st = st * dec_ref[idx * c:idx * c + 1, head(hh)] + ds_ref[idx * hg + hh]
            st_ref[s * hg + hh] = st
            sout_ref[s, hh] = st.T
    for idx in range(nseq * nchunk):
        lo = idx * c
        for hh in range(hg):
            qi = qin_ref[lo:lo + c, head(hh)].astype(BF16)
            o = oin_ref[lo:lo + c, head(hh)] + lax.dot_general(qi, stb_ref[idx * hg + hh], NT,
                                                               preferred_element_type=F32)
            o_ref[lo:lo + c, head(hh)] = o * lax.rsqrt(jnp.mean(o * o, axis=-1, keepdims=True) + EPS)


def _hgrn(ua, lb_param, s0, layer, s0_layer, *, nseq_total, seqlen, nseq, nchunk, c, ha):
    r = ua.shape[0]
    t = nchunk * c
    nsteps = seqlen // t
    rows = nseq * t
    depth = lb_param.shape[0]
    hg = HGRN_HEADS
    assert ha % hg == 0
    w = hg * DKA
    col = lambda off: pl.BlockSpec((rows, w), lambda sb, h, i: (sb * nsteps + i, off // hg + h))
    return pl.pallas_call(
        functools.partial(_hgrn_body, layer=layer, nseq=nseq, nchunk=nchunk, c=c, hg=hg),
        grid=(nseq_total // nseq, ha // hg, nsteps),
        in_specs=[col(0), col(ha), col(2 * ha),
                  pl.BlockSpec((depth, w), lambda sb, h, i: (0, h)),
                  pl.BlockSpec((1, nseq, hg, DKA, DVA), lambda sb, h, i: (s0_layer, sb, h, 0, 0))],
        out_specs=(pl.BlockSpec((rows, w), lambda sb, h, i: (sb * nsteps + i, h)),
                   pl.BlockSpec((nseq, hg, DKA, DVA), lambda sb, h, i: (sb, h, 0, 0))),
        out_shape=(jax.ShapeDtypeStruct((r, ha * DVA), F32),
                   jax.ShapeDtypeStruct((nseq_total, ha, DKA, DVA), F32)),
        scratch_shapes=[pltpu.VMEM((nseq * hg, DVA, DKA), F32)] + [pltpu.VMEM((rows, w), F32)] * 4
                       + [pltpu.VMEM((rows, rows), BF16), pltpu.VMEM((nseq * nchunk * hg, DVA, DKA), F32),
                          pltpu.VMEM((nseq * nchunk * hg, DVA, DKA), BF16)],
        compiler_params=_params(("parallel", "parallel", "arbitrary")), name="hgrn",
    )(ua, ua, ua, lb_param, s0)


def _fox_body(qt_ref, ka_ref, vt_ref, kn_ref, cl_ref, o_ref, m_ref, l_ref, acc_ref, *, tq):
    h = pl.program_id(1)
    i = pl.program_id(2)
    sub = tq // LANES
    qt = _lane_blocks([qt_ref[0, s] for s in range(sub)])
    ones = jnp.where(_iota2((LANES - DH, tq), 0) < N_SPLIT, 1.0, 0.0).astype(BF16)
    q_aug = jnp.concatenate([qt, ones], axis=0)
    qf = qt.astype(F32)
    qn = jnp.sqrt(jnp.sum(qf * qf, axis=0, keepdims=True))

    def live(j):
        bound = qn * kn_ref[j, pl.ds(h, 1), :] - cl_ref[j, pl.ds(h, 1), :] - m_ref[...]
        return jnp.max(bound)

    m_ref[...] = jnp.full_like(m_ref, NEG_BIG)
    l_ref[...] = jnp.zeros_like(l_ref)
    acc_ref[...] = jnp.zeros_like(acc_ref)

    def scores(j, nblk):
        ka = ka_ref[0, pl.ds(_aligned(j * tq, tq), nblk * tq), :]
        return jnp.dot(ka, q_aug, preferred_element_type=F32)

    def consume(s, j, nblk, masked):
        tk = nblk * tq
        vt = _lane_blocks([vt_ref[0, j * sub + t] for t in range(nblk * sub)])
        if masked:
            s = jnp.where(_iota2((tk, tq), 0) <= _iota2((tk, tq), 1), s, -jnp.inf)
        m_old = m_ref[...]
        m_new = jnp.maximum(m_old, jnp.max(s, axis=0, keepdims=True))
        alpha = jnp.exp(m_old - m_new)
        p = jnp.exp(s - m_new)
        l_ref[...] = alpha * l_ref[...] + jnp.sum(p, axis=0, keepdims=True)
        acc_ref[...] = alpha * acc_ref[...] + jnp.dot(vt, p.astype(BF16), preferred_element_type=F32)
        m_ref[...] = m_new

    consume(scores(i, 1), i, 1, True)

    nquad = lax.shift_right_logical(i, 2)

    def cond(c):
        kk, alive = c
        return jnp.logical_and(kk < nquad, alive > FOX_DEAD)

    def body(c):
        kk, _ = c
        ja = i - 2 - 4 * kk
        jb = ja - 2
        sa = scores(ja, 2)
        sb = scores(jb, 2)
        consume(sa, ja, 2, False)
        consume(sb, jb, 2, False)
        return kk + 1, live(jnp.maximum(jb - 1, 0))

    lax.while_loop(cond, body, (0, live(jnp.maximum(i - 1, 0))))
    rest = lax.bitwise_and(i, 3)
    rest_alive = live(jnp.maximum(rest - 1, 0)) > FOX_DEAD

    @pl.when(jnp.logical_and(rest >= 2, rest_alive))
    def _():
        j0 = lax.bitwise_and(rest, 1)
        consume(scores(j0, 2), j0, 2, False)

    @pl.when(jnp.logical_and(lax.bitwise_and(rest, 1) == 1, rest_alive))
    def _():
        consume(scores(0, 1), 0, 1, False)

    o_ref[0] = _rms_rows_to_cols(acc_ref[...] / l_ref[...])


def _fox(qt, ka, vt, kn, cl, *, nseq, seqlen):
    nh, r, _ = ka.shape
    tq = ROW_TILE
    nq = seqlen // tq
    sub = tq // LANES
    return pl.pallas_call(
        functools.partial(_fox_body, tq=tq),
        grid=(nseq, nh, nq),
        in_specs=[pl.BlockSpec((1, sub, DH, LANES), lambda b, h, i: (h, b * nq + i, 0, 0)),
                  pl.BlockSpec((1, seqlen, LANES), lambda b, h, i: (h, b, 0)),
                  pl.BlockSpec((1, seqlen // LANES, DH, LANES), lambda b, h, i: (h, b, 0, 0)),
                  pl.BlockSpec((nq, 8, tq), lambda b, h, i: (b, 0, 0)),
                  pl.BlockSpec((nq, 8, tq), lambda b, h, i: (b, 0, 0))],
        out_specs=pl.BlockSpec((1, tq, DH), lambda b, h, i: (h, b * nq + i, 0)),
        out_shape=jax.ShapeDtypeStruct((nh, r, DH), F32),
        scratch_shapes=[pltpu.VMEM((1, tq), F32), pltpu.VMEM((1, tq), F32), pltpu.VMEM((DH, tq), F32)],
        compiler_params=_params(("parallel", "parallel", "parallel")), name="fox",
    )(qt, ka, vt, kn, cl)


def _sb_body(qt_ref, k_ref, vt_ref, o_ref, carry_ref, acc_ref, *, tq):
    i = pl.program_id(2)
    ts = SB_SUB
    nsub = tq // ts
    per = ts // LANES
    qt = _lane_blocks([qt_ref[0, s] for s in range(tq // LANES)])
    later = _mask01(_iota2((ts, ts), 1) > _iota2((ts, ts), 0))
    later2 = jnp.concatenate([later, later], axis=1)

    def window(jj0, nw, lo, width, masked):
        rows = nw * ts
        lanes = slice(lo, lo + width)
        k = k_ref[0, pl.ds(_aligned(jj0 * ts, ts), rows), :]
        vt = _lane_blocks([vt_ref[0, jj0 * per + s] for s in range(nw * per)])
        z = jnp.dot(k, qt[:, lanes], preferred_element_type=F32)
        if masked:
            ahead = (_iota2((rows, width), 0) - _iota2((rows, width), 1)) < (i * tq + lo - jj0 * ts)
            z = jnp.where(ahead, z, NEG_BIG)
        sp = _softplus_scores(z)
        pieces = []
        newer = carry_ref[:, lanes]
        for s in reversed(range(nw)):
            sp_s = sp[s * ts:(s + 1) * ts, :]
            hi, lo_term = _split_bf16(sp_s, 2)
            behind = jnp.dot(later2, jnp.concatenate([hi, lo_term], axis=0), preferred_element_type=F32)
            pieces.append(behind + newer)
            newer = newer + behind[0:1, :] + sp_s[0:1, :]
        behind = pieces[0] if nw == 1 else jnp.concatenate(pieces[::-1], axis=0)
        w = jnp.exp(z - sp - behind)
        acc_ref[:, lanes] += jnp.dot(vt, w.astype(BF16), preferred_element_type=F32)
        carry_ref[:, lanes] = newer

    assert nsub == 2
    base = jnp.maximum(i * nsub - 1, 0)
    half = tq // 2
    tiles = [(0, 0), (1, 0), (0, half), (1, half), (2, half)]
    k_win = k_ref[0, pl.ds(_aligned(base * ts, ts), (nsub + 1) * ts), :]
    z_lo = jnp.dot(k_win[:nsub * ts], qt[:, :half], preferred_element_type=F32)
    z_hi = jnp.dot(k_win, qt[:, half:], preferred_element_type=F32)
    diff = _iota2((ts, half), 0) - _iota2((ts, half), 1)
    z_tiles = []
    for s, lo in tiles:
        zt = (z_lo if lo == 0 else z_hi)[s * ts:(s + 1) * ts, :]
        z_tiles.append(jnp.where(diff < (i * tq + lo - (base + s) * ts), zt, NEG_BIG))
    z = jnp.concatenate(z_tiles, axis=1)
    sp = _softplus_scores(z)
    hi, lo_term = _split_bf16(sp, 2)
    behind = jnp.dot(later2, jnp.concatenate([hi, lo_term], axis=0), preferred_element_type=F32)
    mass = behind[0:1, :] + sp[0:1, :]
    col = lambda t: slice(t * half, (t + 1) * half)
    zero = jnp.zeros((1, half), F32)
    newer = jnp.concatenate([mass[:, col(1)], zero,
                             mass[:, col(3)] + mass[:, col(4)], mass[:, col(4)], zero], axis=1)
    w = jnp.exp(z - sp - behind - newer).astype(BF16)
    out = [None, None]
    for t, (s, lo) in enumerate(tiles):
        vt = _lane_blocks([vt_ref[0, (base + s) * per + u] for u in range(per)])
        term = jnp.dot(vt, w[:, col(t)], preferred_element_type=F32)
        side = 0 if lo == 0 else 1
        out[side] = term if out[side] is None else out[side] + term
    acc_ref[...] = jnp.concatenate(out, axis=1)
    carry_ref[...] = jnp.concatenate([mass[:, col(0)] + mass[:, col(1)],
                                      mass[:, col(2)] + mass[:, col(3)] + mass[:, col(4)]], axis=1)

    for lo in (0, half):
        def cond(c):
            jj, live = c
            return jnp.logical_and(jj >= 0, live < -SB_DEAD)

        def body(c, lo=lo):
            jj, _ = c
            window(jj, 1, lo, half, False)
            return jj - 1, jnp.min(carry_ref[:, lo:lo + half])

        lax.while_loop(cond, body, (i * nsub - 2, jnp.min(carry_ref[:, lo:lo + half])))
    o_ref[0] = _rms_rows_to_cols(acc_ref[...])


def _sb(qt, k, vt, *, nseq, seqlen):
    nh, r, _ = k.shape
    tq = ROW_TILE
    nq = seqlen // tq
    assert seqlen >= tq + SB_SUB
    return pl.pallas_call(
        functools.partial(_sb_body, tq=tq),
        grid=(nseq, nh, nq),
        in_specs=[pl.BlockSpec((1, tq // LANES, DH, LANES), lambda b, h, i: (h, b * nq + i, 0, 0)),
                  pl.BlockSpec((1, seqlen, DH), lambda b, h, i: (h, b, 0)),
                  pl.BlockSpec((1, seqlen // LANES, DH, LANES), lambda b, h, i: (h, b, 0, 0))],
        out_specs=pl.BlockSpec((1, tq, DH), lambda b, h, i: (h, b * nq + i, 0)),
        out_shape=jax.ShapeDtypeStruct((nh, r, DH), F32),
        scratch_shapes=[pltpu.VMEM((1, tq), F32), pltpu.VMEM((DH, tq), F32)],
        compiler_params=_params(("parallel", "parallel", "parallel")), name="stickbreak",
    )(qt, k, vt)


def _dec_body(pt_ref, fk_hbm, fv_hbm, lf_hbm, ck_hbm, cv_hbm,
              qb_ref, kb_ref, vb_ref, lfn_ref, qc_ref, kc_ref, vc_ref,
              ob_ref, oc_ref,
              fk_buf, fv_buf, lf_buf, ck_buf, cv_buf, xk_buf, xv_buf, o_acc, carry_ref, sem_pre, sem_x,
              *, layer, npages, page, nq, hb, hc, group):
    n = npages
    npre = min(SB_PREFETCH, n)
    step = pl.program_id(0)
    slot = lax.rem(step, 2)

    def prefetch(step, slot):
        out = []
        for g in range(group):
            seq = step * group + g
            for p in range(n):
                pg = pt_ref[seq, p]
                dst = pl.ds(p * page, page)
                out.append(pltpu.make_async_copy(fk_hbm.at[layer, pg], fk_buf.at[slot, g, :, dst], sem_pre.at[slot]))
                out.append(pltpu.make_async_copy(fv_hbm.at[layer, pg], fv_buf.at[slot, g, :, dst], sem_pre.at[slot]))
                out.append(pltpu.make_async_copy(lf_hbm.at[layer, pg], lf_buf.at[slot, g, p], sem_pre.at[slot]))
            for t in range(npre):
                pg = pt_ref[seq, n - npre + t]
                dst = pl.ds(t * page, page)
                out.append(pltpu.make_async_copy(ck_hbm.at[layer, pg], ck_buf.at[slot, g, :, dst], sem_pre.at[slot]))
                out.append(pltpu.make_async_copy(cv_hbm.at[layer, pg], cv_buf.at[slot, g, :, dst], sem_pre.at[slot]))
        return out

    @pl.when(step == 0)
    def _():
        for cp in prefetch(0, 0):
            cp.start()

    @pl.when(step + 1 < pl.num_programs(0))
    def _():
        for cp in prefetch(step + 1, 1 - slot):
            cp.start()

    for cp in prefetch(step, slot):
        cp.wait()

    def q_rows(q, nh):
        w = nh * DH
        qt = jnp.concatenate([q] * nh, axis=0)
        keep = _div_pow2(_iota2((nh * nq, w), 1), DH) == _div_pow2(_iota2((nh * nq, w), 0), nq)
        return jnp.where(keep, qt, 0.0).astype(BF16)

    def new_rows(ref, g):
        return ref[g * nq:(g + 1) * nq, :]

    def new_page(ref, g):
        x = new_rows(ref, g)
        return jnp.concatenate([x, jnp.zeros((page - nq, x.shape[1]), F32)], axis=0).astype(BF16)

    def per_query_rows(c):
        return jnp.concatenate([jnp.broadcast_to(c[i:i + 1, :], (nq, c.shape[1])) for i in range(c.shape[0])], axis=0)

    def stack(per_seq):
        return per_seq[0] if group == 1 else jnp.concatenate(per_seq, axis=0)

    def write_heads(o_full, nh, o_ref, g):
        for hh in range(nh):
            blk = o_full[hh * nq:(hh + 1) * nq, hh * DH:(hh + 1) * DH]
            o_ref[hh, g * nq:(g + 1) * nq, :] = blk * lax.rsqrt(jnp.mean(blk * blk, axis=-1, keepdims=True) + EPS)

    assert hb == hc
    rows = hb * nq
    upper = _mask01(_iota2((page, page), 0) <= _iota2((page, page), 1))
    later = _mask01(_iota2((page, page), 0) > _iota2((page, page), 1))
    key = _iota2((group * rows, page), 1)
    q_of_row = _mod_pow2(_iota2((group * rows, page), 0), nq)

    qr = [q_rows(new_rows(qb_ref, g), hb) for g in range(group)]
    lf_all = jnp.concatenate([lf_buf[slot, g, p] for p in range(n) for g in range(group)], axis=0)
    within = _dot_exact_lhs(lf_all, upper)
    carry = jnp.zeros((group * hb, 1), F32)
    c_pages = []
    for p in range(n):
        c = within[p * group * hb:(p + 1) * group * hb, :] + carry
        carry = c[:, page - 1:page]
        c_pages.append(per_query_rows(c))
    pick = _mask01(_iota2((rows, 128), 1) == _div_pow2(_iota2((rows, 128), 0), nq))
    c_new = []
    for g in range(group):
        lf_new = new_rows(lfn_ref, g)
        run = [lf_new[0:1, :]]
        for t in range(1, nq):
            run.append(run[-1] + lf_new[t:t + 1, :])
        c_rows = jnp.concatenate(run + [jnp.zeros((page - nq, lf_new.shape[1]), F32)], axis=0)
        acc = None
        for part in _split_bf16(c_rows, 3):
            term = lax.dot_general(pick, part, NT, preferred_element_type=F32)
            acc = term if acc is None else acc + term
        c_new.append(acc)
    c_new = stack(c_new) + per_query_rows(carry)
    s_past = stack([jnp.dot(qr[g], fk_buf[slot, g].astype(BF16), preferred_element_type=F32) for g in range(group)])
    s_past = s_past - jnp.concatenate(c_pages, axis=1)
    s_new = stack([lax.dot_general(qr[g], new_page(kb_ref, g), NT, preferred_element_type=F32)
                   for g in range(group)]) - c_new
    s_new = jnp.where(key <= q_of_row, s_new, -jnp.inf)
    m = jnp.maximum(jnp.max(s_past, axis=-1, keepdims=True), jnp.max(s_new, axis=-1, keepdims=True))
    p_past = jnp.exp(s_past - m)
    p_new = jnp.exp(s_new - m)
    inv_l = 1.0 / (jnp.sum(p_past, axis=-1, keepdims=True) + jnp.sum(p_new, axis=-1, keepdims=True))
    p_past = p_past.astype(BF16)
    p_new = p_new.astype(BF16)
    for g in range(group):
        sl = slice(g * rows, (g + 1) * rows)
        o = lax.dot_general(p_past[sl], fv_buf[slot, g].astype(BF16), NT, preferred_element_type=F32)
        o = o + jnp.dot(p_new[sl], new_page(vb_ref, g), preferred_element_type=F32)
        write_heads(o * inv_l[sl], hb, ob_ref, g)

    def sb_pages(z_pages, v_of_page, mask_newest, carry_in):
        out = None
        newer = carry_in
        r = z_pages[0].shape[0]
        for idx in reversed(range(len(z_pages))):
            z = z_pages[idx]
            sp = _softplus_scores(z)
            masked = mask_newest and idx == len(z_pages) - 1
            log_keep = jnp.where(key[:r] < q_of_row[:r], -sp, 0.0) if masked else -sp
            after = _dot_exact_lhs(log_keep, later, n=2) + newer
            w = jnp.exp(z - sp + after)
            if masked:
                w = jnp.where(key[:r] < q_of_row[:r], w, 0.0)
            term = v_of_page(idx, w.astype(BF16))
            out = term if out is None else out + term
            newer = newer + jnp.sum(log_keep, axis=-1, keepdims=True)
        return out, newer

    qr = [q_rows(new_rows(qc_ref, g), hc) for g in range(group)]
    cv = [cv_buf[slot, g].astype(BF16) for g in range(group)]
    v_new = [new_page(vc_ref, g) for g in range(group)]
    z_pre = stack([jnp.dot(qr[g], ck_buf[slot, g].astype(BF16), preferred_element_type=F32) for g in range(group)])
    z_pages = [z_pre[:, t * page:(t + 1) * page] for t in range(npre)]
    z_pages.append(stack([lax.dot_general(qr[g], new_page(kc_ref, g), NT, preferred_element_type=F32)
                          for g in range(group)]))

    def v_pre(idx, w):
        outs = []
        for g in range(group):
            wg = w[g * rows:(g + 1) * rows]
            if idx == npre:
                outs.append(jnp.dot(wg, v_new[g], preferred_element_type=F32))
            else:
                outs.append(lax.dot_general(wg, cv[g][:, idx * page:(idx + 1) * page], NT,
                                            preferred_element_type=F32))
        return stack(outs)

    o, carry = sb_pages(z_pages, v_pre, True, jnp.zeros((group * rows, 1), F32))
    o_acc[...] = o
    carry_ref[...] = carry

    for g in range(group):
        sl = slice(g * rows, (g + 1) * rows)
        seq = step * group + g

        def cond(c):
            p, live = c
            return jnp.logical_and(p >= 0, live > SB_DEAD)

        def body(c, g=g, sl=sl, seq=seq):
            p, _ = c
            pg = pt_ref[seq, p]
            cp_k = pltpu.make_async_copy(ck_hbm.at[layer, pg], xk_buf, sem_x.at[0])
            cp_v = pltpu.make_async_copy(cv_hbm.at[layer, pg], xv_buf, sem_x.at[1])
            cp_k.start()
            cp_v.start()
            cp_k.wait()
            cp_v.wait()
            z = jnp.dot(qr[g], xk_buf[...].astype(BF16), preferred_element_type=F32)
            xv = xv_buf[...].astype(BF16)
            o_p, newer = sb_pages([z], lambda idx, w: lax.dot_general(w, xv, NT, preferred_element_type=F32),
                                  False, carry_ref[sl, :])
            o_acc[sl, :] += o_p
            carry_ref[sl, :] = newer
            return p - 1, jnp.max(newer)

        lax.while_loop(cond, body, (n - npre - 1, jnp.max(carry_ref[sl, :])))
        write_heads(o_acc[sl, :], hc, oc_ref, g)


def _dec(page_table, layer, cache_fox_k, cache_fox_v, cache_fox_logf, cache_sb_k, cache_sb_v,
         qb, kb, vb, lfr, qc, kc, vc, *, nq):
    nseq, npages = page_table.shape
    page = cache_fox_k.shape[3]
    wb, wc = cache_fox_k.shape[2], cache_sb_k.shape[2]
    hb, hc = wb // DH, wc // DH
    npre = min(SB_PREFETCH, npages)
    group = DEC_GROUP
    assert nseq % group == 0
    hbm = pl.BlockSpec(memory_space=pl.ANY)
    new = lambda width: pl.BlockSpec((group * nq, width), lambda b, pt: (b, 0))
    heads = lambda nh: pl.BlockSpec((nh, group * nq, DH), lambda b, pt: (0, b, 0))
    grid_spec = pltpu.PrefetchScalarGridSpec(
        num_scalar_prefetch=1, grid=(nseq // group,),
        in_specs=[hbm] * 5 + [new(wb), new(wb), new(wb), new(128), new(wc), new(wc), new(wc)],
        out_specs=(heads(hb), heads(hc)),
        scratch_shapes=[pltpu.VMEM((2, group, wb, npages * page), F32),
                        pltpu.VMEM((2, group, wb, npages * page), F32),
                        pltpu.VMEM((2, group, npages, hb, page), F32),
                        pltpu.VMEM((2, group, wc, npre * page), F32), pltpu.VMEM((2, group, wc, npre * page), F32),
                        pltpu.VMEM((wc, page), F32), pltpu.VMEM((wc, page), F32),
                        pltpu.VMEM((group * hc * nq, wc), F32), pltpu.VMEM((group * hc * nq, 1), F32),
                        pltpu.SemaphoreType.DMA((2,)), pltpu.SemaphoreType.DMA((2,))])
    return pl.pallas_call(
        functools.partial(_dec_body, layer=layer, npages=npages, page=page, nq=nq, hb=hb, hc=hc, group=group),
        grid_spec=grid_spec,
        out_shape=(jax.ShapeDtypeStruct((hb, nseq * nq, DH), F32),
                   jax.ShapeDtypeStruct((hc, nseq * nq, DH), F32)),
        compiler_params=_params(("arbitrary",)), name="paged_decode",
    )(page_table, cache_fox_k, cache_fox_v, cache_fox_logf, cache_sb_k, cache_sb_v, qb, kb, vb, lfr, qc, kc, vc)


def _out_proj_body(oa_ref, ob_ref, oc_ref, gate_ref, w_ref, x_ref, gpost_ref, y_ref):
    parts = [oa_ref[...]]
    parts += [ob_ref[hh] for hh in range(ob_ref.shape[0])]
    parts += [oc_ref[hh] for hh in range(oc_ref.shape[0])]
    o = jnp.concatenate(parts, axis=-1) * gate_ref[...]
    y = jnp.dot(o.astype(BF16), w_ref[...], preferred_element_type=F32)
    y = y * lax.rsqrt(jnp.mean(y * y, axis=-1, keepdims=True) + EPS) * gpost_ref[...]
    y_ref[...] = x_ref[...] + y


def _out_proj(oa, ob, oc, gate, w, x2d, gpost):
    r, d = x2d.shape
    tm = ROW_TILE
    rows = lambda width: pl.BlockSpec((tm, width), lambda i: (i, 0))
    heads = lambda nh: pl.BlockSpec((nh, tm, DH), lambda i: (0, i, 0))
    full = lambda shape: pl.BlockSpec(shape, lambda i: (0,) * len(shape))
    return pl.pallas_call(
        _out_proj_body, grid=(r // tm,),
        in_specs=[rows(oa.shape[1]), heads(ob.shape[0]), heads(oc.shape[0]), rows(gate.shape[1]),
                  full(w.shape), rows(d), full((1, d))],
        out_specs=rows(d), out_shape=jax.ShapeDtypeStruct((r, d), F32),
        compiler_params=_params(("parallel",)), name="out_proj",
    )(oa, ob, oc, gate, w, x2d, gpost)


def kernel(x_prompt, x_sample, cache_fox_k, cache_fox_v, cache_fox_logf, cache_sb_k, cache_sb_v,
           state_hgrn, page_table, w_in, b_fox, lb_param, w_out, g_pre, g_post, g_out):
    depth, d, n_in = w_in.shape
    mix = w_out.shape[1]
    hb = b_fox.shape[1]
    wa = lb_param.shape[1]
    wb = hb * DH
    wc = mix - wa - wb
    hc = wc // DH
    ha = wa // DKA
    assert n_in == 4 * wa + 4 * wb + hb + 4 * wc
    bp, lp, _ = x_prompt.shape
    bs, ls, _ = x_sample.shape
    n_phys, page = cache_fox_k.shape[1], cache_fox_k.shape[2]
    assert lp % ROW_TILE == 0 and (bs * ls) % ROW_TILE == 0 and HGRN_CHUNK % ls == 0

    o_qb = 4 * wa
    o_fb = o_qb + 4 * wb
    o_qc = o_fb + hb
    w_perm = jnp.concatenate([
        w_in[:, :, 0:3 * wa], w_in[:, :, 3 * wa:4 * wa],
        w_in[:, :, o_qb + 3 * wb:o_qb + 4 * wb], w_in[:, :, o_qc + 3 * wc:o_qc + 4 * wc],
        w_in[:, :, o_qb:o_qb + 3 * wb], w_in[:, :, o_qc:o_qc + 3 * wc]], axis=-1).astype(BF16)
    w_f = w_in[:, :, o_fb:o_fb + hb]
    w_fr = jnp.pad(w_f, ((0, 0), (0, 0), (0, 128 - hb))).astype(BF16)
    w_ft = jnp.pad(jnp.swapaxes(w_f, 1, 2), ((0, 0), (0, 16 - hb), (0, 0))).astype(BF16)
    b_fr = jnp.pad(b_fox, ((0, 0), (0, 128 - hb)))[:, None, :]
    b_ft = jnp.pad(b_fox, ((0, 0), (0, 16 - hb)))[:, :, None]
    w_out_b = w_out.astype(BF16)

    pages_t = [jnp.transpose(a, (0, 1, 3, 4, 2)).reshape(depth, n_phys, -1, page)
               for a in (cache_fox_k, cache_fox_v, cache_sb_k, cache_sb_v)]
    logf_t = jnp.transpose(cache_fox_logf, (0, 1, 3, 2))
    zero_state = jnp.zeros((1, bp, ha, DKA, DVA), F32)

    xp = x_prompt.reshape(bp * lp, d)
    xs = x_sample.reshape(bs * ls, d)
    p_out = [[] for _ in range(6)]
    s_out = [[] for _ in range(6)]
    for l in range(depth):
        proj = functools.partial(_in_proj, gpre=g_pre[l][None, :], w=w_perm[l], gout=g_out[l][None, :],
                                 wa=wa, wb=wb, wc=wc)

        (ua, gate, qbt, kbp, vbt, kbt32, vbt32, qct, kcr, vct, kct32, vct32, lft) = proj(
            xp, wf=w_ft[l], bf=b_ft[l], prompt=True, nseq=bp)
        oa, s_fin = _hgrn(ua, lb_param, zero_state, l, 0, nseq_total=bp, seqlen=lp, nseq=1,
                          nchunk=ROW_TILE // HGRN_CHUNK, c=HGRN_CHUNK, ha=ha)
        kba, kn, cl = _cum(lft, kbp, bp, lp)
        ob = _fox(qbt, kba, vbt, kn, cl, nseq=bp, seqlen=lp)
        oc = _sb(qct, kcr, vct, nseq=bp, seqlen=lp)
        xp = _out_proj(oa, ob, oc, gate, w_out_b[l], xp, g_post[l][None, :])
        for lst, a in zip(p_out, (kbt32, vbt32, lft[:hb].reshape(hb, bp, lp), kct32, vct32, s_fin)):
            lst.append(a)

        ua, gate, qb, kb, vb, qc, kc, vc, lfr = proj(xs, wf=w_fr[l], bf=b_fr[l], prompt=False)
        oa, s_fin = _hgrn(ua, lb_param, state_hgrn, l, l, nseq_total=bs, seqlen=ls, nseq=16,
                          nchunk=1, c=ls, ha=ha)
        ob, oc = _dec(page_table, l, pages_t[0], pages_t[1], logf_t, pages_t[2], pages_t[3],
                      qb, kb, vb, lfr, qc, kc, vc, nq=ls)
        xs = _out_proj(oa, ob, oc, gate, w_out_b[l], xs, g_post[l][None, :])
        for lst, a in zip(s_out, (kb.reshape(bs, ls, hb, DH), vb.reshape(bs, ls, hb, DH),
                                  lfr[:, :hb].reshape(bs, ls, hb), kc.reshape(bs, ls, hc, DH),
                                  vc.reshape(bs, ls, hc, DH), s_fin)):
            lst.append(a)

    p_st = [jnp.stack(a, axis=0) for a in p_out]
    s_st = [jnp.stack(a, axis=0) for a in s_out]
    for idx in (0, 1, 3, 4):
        p_st[idx] = jnp.transpose(p_st[idx], (0, 1, 4, 2, 3))
    p_st[2] = jnp.transpose(p_st[2], (0, 2, 3, 1))
    return (xp.reshape(bp, lp, d), xs.reshape(bs, ls, d), *p_st, *s_st)
```
